```python
import math
import jax
import jax.numpy as jnp
from jax import lax
import numpy as np


D_MODEL = 1024
BATCH = 8
SEQ = 4096
DEPTH = 4

N_MIXERS = 4
HEAD_DIM = 64
N_HEADS = D_MODEL // HEAD_DIM
ROT_DIM = HEAD_DIM // 4
ROPE_THETA = 500000.0
Q_BLOCK = 128
D_FF = 2816
ALPHA = (2 * DEPTH) ** 0.25
BETA = (8 * DEPTH) ** -0.25
N_ADA = 9
LN_EPS = 1e-5
RMS_EPS = 1e-6

A_KV_RANK = 128
A_IDX_HEADS = 8
A_IDX_DIM = 32
A_TOPK_MAX = 256
A_IN = D_MODEL + A_KV_RANK + A_IDX_HEADS * A_IDX_DIM + A_IDX_DIM + A_IDX_HEADS
B_IN = 3 * D_MODEL + N_HEADS
C_KV_HEADS = 2
C_GROUP = N_HEADS // C_KV_HEADS
C_WINDOW = 128
C_IN = D_MODEL + 2 * C_KV_HEADS * HEAD_DIM
D_HEADS = N_HEADS // 2
D_QK_DIM = HEAD_DIM
D_V_DIM = 2 * HEAD_DIM
D_IN = 2 * D_HEADS * 2 * D_QK_DIM + D_HEADS * D_V_DIM

kernel_name = 'hybrid_interleaved_dsa_fox_swa_diff_macaron'


def layer_norm(x, g, b):
    xf = x.astype(jnp.float32)
    mu = jnp.mean(xf, axis=-1, keepdims=True)
    var = jnp.mean(jnp.square(xf - mu), axis=-1, keepdims=True)
    return ((xf - mu) * lax.rsqrt(var + LN_EPS) * g + b).astype(x.dtype)


def rms_norm(x, g):
    xf = x.astype(jnp.float32)
    return (xf * lax.rsqrt(jnp.mean(xf * xf, axis=-1, keepdims=True) + RMS_EPS) * g).astype(x.dtype)


def rope_tables(seq, rot_dim):
    inv = ROPE_THETA ** (-jnp.arange(0, rot_dim, 2, dtype=jnp.float32) / rot_dim)
    ang = jnp.arange(seq, dtype=jnp.float32)[:, None] * inv[None, :]
    return jnp.cos(ang), jnp.sin(ang)


def partial_rope(x, cos, sin):
    half = cos.shape[-1]
    x1, x2, rest = x[..., :half], x[..., half:2 * half], x[..., 2 * half:]
    c = cos[:, None, :].astype(x.dtype)
    s = sin[:, None, :].astype(x.dtype)
    return jnp.concatenate([x1 * c - x2 * s, x2 * c + x1 * s, rest], axis=-1)


def to_blocks(t):
    b, s = t.shape[:2]
    return jnp.moveaxis(t.reshape(b, s // Q_BLOCK, Q_BLOCK, *t.shape[2:]), 1, 0)


def from_blocks(t):
    nb, b, q = t.shape[:3]
    return jnp.moveaxis(t, 0, 1).reshape(b, nb * q, *t.shape[3:])


def swiglu(h, w_in, w_out):
    g, u = jnp.split(h @ w_in, 2, axis=-1)
    return (jax.nn.silu(g) * u) @ w_out


def mixer_dsa(h, w_in, kv_norm, w_kv_up, w_out, cos, sin, cos_i, sin_i):
    b, s, _ = h.shape
    k_sel = min(A_TOPK_MAX, s // 4)
    splits = [D_MODEL, D_MODEL + A_KV_RANK, D_MODEL + A_KV_RANK + A_IDX_HEADS * A_IDX_DIM,
              D_MODEL + A_KV_RANK + A_IDX_HEADS * A_IDX_DIM + A_IDX_DIM]
    q, c_kv, q_idx, k_idx, w_idx = jnp.split(h @ w_in, splits, axis=-1)
    q = partial_rope(q.reshape(b, s, N_HEADS, HEAD_DIM), cos, sin)
    kv = rms_norm(c_kv, kv_norm) @ w_kv_up
    k = partial_rope(kv[:, :, None, :HEAD_DIM], cos, sin)[:, :, 0]
    v = kv[..., HEAD_DIM:]
    q_idx = partial_rope(q_idx.reshape(b, s, A_IDX_HEADS, A_IDX_DIM), cos_i, sin_i)
    k_idx = partial_rope(k_idx[:, :, None, :], cos_i, sin_i)[:, :, 0]
    w_idx = w_idx * (A_IDX_HEADS ** -0.5 * A_IDX_DIM ** -0.5)
    key_pos = jnp.arange(s)

    def block(args):
        qx, qix, wix, qpos = args
        rel = jax.nn.relu(jnp.einsum('bqhd,bkd->bqhk', qix, k_idx))
        score = jnp.einsum('bqh,bqhk->bqk', wix, rel).astype(jnp.float32)
        score = jnp.where((key_pos[None, :] <= qpos[:, None])[None], score, -jnp.inf)
        _, sel = lax.top_k(score, k_sel)
        valid = sel <= qpos[None, :, None]
        kg = jax.vmap(lambda kb, ib: kb[ib])(k, sel)
        vg = jax.vmap(lambda vb, ib: vb[ib])(v, sel)
        logits = jnp.einsum('bqhd,bqkd->bhqk', qx, kg).astype(jnp.float32) * HEAD_DIM ** -0.5
        logits = jnp.where(valid[:, None], logits, -jnp.inf)
        p = jax.nn.softmax(logits, axis=-1).astype(vg.dtype)
        return jnp.einsum('bhqk,bqkd->bqhd', p, vg)

    out = lax.map(block, (to_blocks(q), to_blocks(q_idx), to_blocks(w_idx), key_pos.reshape(-1, Q_BLOCK)))
    return from_blocks(out).reshape(b, s, D_MODEL) @ w_out


def mixer_fox(h, w_in, f_bias, w_out):
    b, s, _ = h.shape
    q, k, v, f = jnp.split(h @ w_in, [D_MODEL, 2 * D_MODEL, 3 * D_MODEL], axis=-1)
    q = q.reshape(b, s, N_HEADS, HEAD_DIM)
    k = k.reshape(b, s, N_HEADS, HEAD_DIM)
    v = v.reshape(b, s, N_HEADS, HEAD_DIM)
    log_f = jax.nn.log_sigmoid((f + f_bias).astype(jnp.float32))
    cum = jnp.cumsum(log_f, axis=1)
    cum_k = jnp.moveaxis(cum, 1, 2)
    key_pos = jnp.arange(s)

    def block(args):
        qx, cq, qpos = args
        logits = jnp.einsum('bqhd,bkhd->bhqk', qx, k).astype(jnp.float32) * HEAD_DIM ** -0.5
        logits = logits + jnp.moveaxis(cq, 1, 2)[..., None] - cum_k[:, :, None, :]
        logits = jnp.where(key_pos[None, :] <= qpos[:, None], logits, -jnp.inf)
        p = jax.nn.softmax(logits, axis=-1).astype(v.dtype)
        return jnp.einsum('bhqk,bkhd->bqhd', p, v)

    out = lax.map(block, (to_blocks(q), to_blocks(cum), key_pos.reshape(-1, Q_BLOCK)))
    return from_blocks(out).reshape(b, s, D_MODEL) @ w_out


def mixer_swa(h, w_in, sinks, w_out, cos, sin):
    b, s, _ = h.shape
    nb = s // Q_BLOCK
    q, k, v = jnp.split(h @ w_in, [D_MODEL, D_MODEL + C_KV_HEADS * HEAD_DIM], axis=-1)
    q = partial_rope(q.reshape(b, s, N_HEADS, HEAD_DIM), cos, sin)
    k = partial_rope(k.reshape(b, s, C_KV_HEADS, HEAD_DIM), cos, sin)
    v = v.reshape(b, s, C_KV_HEADS, HEAD_DIM)

    def band(t):
        tb = t.reshape(b, nb, Q_BLOCK, C_KV_HEADS, HEAD_DIM)
        prev = jnp.pad(tb, ((0, 0), (1, 0), (0, 0), (0, 0), (0, 0)))[:, :-1]
        return jnp.moveaxis(jnp.concatenate([prev, tb], axis=2), 1, 0)

    qb = jnp.moveaxis(q.reshape(b, nb, Q_BLOCK, C_KV_HEADS, C_GROUP, HEAD_DIM), 1, 0)
    qi = jnp.arange(Q_BLOCK)[:, None] + Q_BLOCK
    ki = jnp.arange(2 * Q_BLOCK)[None, :]
    in_window = (ki <= qi) & (qi - ki < C_WINDOW)
    sink = sinks.reshape(C_KV_HEADS, C_GROUP).astype(jnp.float32)

    def block(args):
        qx, kx, vx, n = args
        logits = jnp.einsum('bqgrd,bkgd->bgrqk', qx, kx).astype(jnp.float32) * HEAD_DIM ** -0.5
        ok = in_window & ((ki >= Q_BLOCK) | (n > 0))
        logits = jnp.where(ok, logits, -jnp.inf)
        sink_col = jnp.broadcast_to(sink[None, :, :, None, None], logits.shape[:-1] + (1,))
        p = jax.nn.softmax(jnp.concatenate([logits, sink_col], axis=-1), axis=-1)[..., :-1]
        return jnp.einsum('bgrqk,bkgd->bqgrd', p.astype(vx.dtype), vx)

    out = lax.map(block, (qb, band(k), band(v), jnp.arange(nb)))
    return jnp.moveaxis(out, 0, 1).reshape(b, s, D_MODEL) @ w_out


def mixer_diff(h, w_in, lam, subln_g, w_out, cos, sin, lambda_init):
    b, s, _ = h.shape
    qk_w = D_HEADS * 2 * D_QK_DIM
    q, k, v = jnp.split(h @ w_in, [qk_w, 2 * qk_w], axis=-1)
    q = partial_rope(q.reshape(b, s, 2 * D_HEADS, D_QK_DIM), cos, sin).reshape(b, s, D_HEADS, 2, D_QK_DIM)
    k = partial_rope(k.reshape(b, s, 2 * D_HEADS, D_QK_DIM), cos, sin).reshape(b, s, D_HEADS, 2, D_QK_DIM)
    v = v.reshape(b, s, D_HEADS, D_V_DIM)
    lam_f = lam.astype(jnp.float32)
    lam_val = (jnp.exp(jnp.sum(lam_f[0] * lam_f[1])) - jnp.exp(jnp.sum(lam_f[2] * lam_f[3])) + lambda_init)
    key_pos = jnp.arange(s)

    def block(args):
        qx, qpos = args
        logits = jnp.einsum('bqhcd,bkhcd->bhcqk', qx, k).astype(jnp.float32) * D_QK_DIM ** -0.5
        logits = jnp.where(key_pos[None, :] <= qpos[:, None], logits, -jnp.inf)
        p = jax.nn.softmax(logits, axis=-1)
        a = p[:, :, 0] - lam_val * p[:, :, 1]
        return jnp.einsum('bhqk,bkhd->bqhd', a.astype(v.dtype), v)

    out = from_blocks(lax.map(block, (to_blocks(q), key_pos.reshape(-1, Q_BLOCK))))
    out = rms_norm(out, subln_g) * (1.0 - lambda_init)
    return out.reshape(b, s, D_HEADS * D_V_DIM) @ w_out


def modulated_post_norm(x, mod, j, fn, res_w, g, b):
    shift, scale, gate = mod[:, 3 * j], mod[:, 3 * j + 1], mod[:, 3 * j + 2]
    y = fn(x * (1.0 + scale) + shift)
    return layer_norm(ALPHA * x + res_w * (1.0 + gate) * y, g, b)


def setup_inputs(seed: int = 0) -> dict:
    key = jax.random.key(seed)
    ks = iter(jax.random.split(key, 32))

    def nrm(shape, scale):
        return jax.random.normal(next(ks), shape, jnp.float32) * scale

    n_of = [len(range(m, DEPTH, N_MIXERS)) for m in range(N_MIXERS)]
    na, nf, nw, nd = n_of
    d = D_MODEL
    return {
        'x': nrm((BATCH, SEQ, d), 1.0),
        'c': nrm((BATCH, d), 1.0),
        'ln_g': 1.0 + nrm((DEPTH, 3, d), 0.05),
        'ln_b': nrm((DEPTH, 3, d), 0.02),
        'w_ada': nrm((DEPTH, d, N_ADA * d), 0.2 * d ** -0.5),
        'b_ada': nrm((DEPTH, N_ADA * d), 0.02),
        'w_ffn_in': nrm((DEPTH, 2, d, 2 * D_FF), d ** -0.5),
        'w_ffn_out': nrm((DEPTH, 2, D_FF, d), BETA * D_FF ** -0.5),
        'dsa_w_in': nrm((na, d, A_IN), d ** -0.5),
        'dsa_kv_norm': 1.0 + nrm((na, A_KV_RANK), 0.05),
        'dsa_w_kv_up': nrm((na, A_KV_RANK, 2 * HEAD_DIM), A_KV_RANK ** -0.5),
        'dsa_w_out': nrm((na, d, d), BETA * d ** -0.5),
        'fox_w_in': nrm((nf, d, B_IN), d ** -0.5),
        'fox_f_bias': jax.random.uniform(next(ks), (nf, N_HEADS), jnp.float32, 1.0, 5.0),
        'fox_w_out': nrm((nf, d, d), BETA * d ** -0.5),
        'swa_w_in': nrm((nw, d, C_IN), d ** -0.5),
        'swa_sinks': nrm((nw, N_HEADS), 0.5),
        'swa_w_out': nrm((nw, d, d), BETA * d ** -0.5),
        'diff_w_in': nrm((nd, d, D_IN), d ** -0.5),
        'diff_lambda': nrm((nd, 4, D_QK_DIM), 0.1),
        'diff_subln': 1.0 + nrm((nd, D_V_DIM), 0.05),
        'diff_w_out': nrm((nd, D_HEADS * D_V_DIM, d), BETA * (D_HEADS * D_V_DIM) ** -0.5),
    }


def reference(x, c, ln_g, ln_b, w_ada, b_ada, w_ffn_in, w_ffn_out,
              dsa_w_in, dsa_kv_norm, dsa_w_kv_up, dsa_w_out,
              fox_w_in, fox_f_bias, fox_w_out,
              swa_w_in, swa_sinks, swa_w_out,
              diff_w_in, diff_lambda, diff_subln, diff_w_out):
    b, s, d = x.shape
    cos, sin = rope_tables(s, ROT_DIM)
    cos_i, sin_i = rope_tables(s, A_IDX_DIM // 4)
    cond = jax.nn.silu(c)
    for i in range(DEPTH):
        m, r = i % N_MIXERS, i // N_MIXERS
        mod = (cond @ w_ada[i] + b_ada[i]).reshape(b, N_ADA, 1, d)
        x = modulated_post_norm(x, mod, 0, lambda h: swiglu(h, w_ffn_in[i, 0], w_ffn_out[i, 0]),
                                0.5, ln_g[i, 0], ln_b[i, 0])
        if m == 0:
            mix = lambda h: mixer_dsa(h, dsa_w_in[r], dsa_kv_norm[r], dsa_w_kv_up[r], dsa_w_out[r],
                                      cos, sin, cos_i, sin_i)
        elif m == 1:
            mix = lambda h: mixer_fox(h, fox_w_in[r], fox_f_bias[r], fox_w_out[r])
        elif m == 2:
            mix = lambda h: mixer_swa(h, swa_w_in[r], swa_sinks[r], swa_w_out[r], cos, sin)
        else:
            lambda_init = 0.8 - 0.6 * math.exp(-0.3 * i)
            mix = lambda h: mixer_diff(h, diff_w_in[r], diff_lambda[r], diff_subln[r], diff_w_out[r],
                                       cos, sin, lambda_init)
        x = modulated_post_norm(x, mod, 1, mix, 1.0, ln_g[i, 1], ln_b[i, 1])
        x = modulated_post_norm(x, mod, 2, lambda h: swiglu(h, w_ffn_in[i, 1], w_ffn_out[i, 1]),
                                0.5, ln_g[i, 2], ln_b[i, 2])
    return x
```

```python
import functools
import math

import jax
import jax.numpy as jnp
from jax import lax
from jax.experimental import pallas as pl
from jax.experimental.pallas import tpu as pltpu

F32 = jnp.float32
BF16 = jnp.bfloat16

HEAD_DIM = 64
ROT_DIM = HEAD_DIM // 4
ROPE_THETA = 500000.0
LN_EPS = 1e-5
RMS_EPS = 1e-6
N_ADA = 9
A_KV_RANK = 128
A_IDX_HEADS = 8
A_IDX_DIM = 32
A_TOPK_MAX = 256
C_WINDOW = 128
LANES = 128
NEG = -1e30
INT_MIN = -(2 ** 31)
VMEM_LIMIT = 56 * 1024 * 1024


def _cparams(n_axes):
    return pltpu.CompilerParams(dimension_semantics=("arbitrary",) * n_axes,
                                vmem_limit_bytes=VMEM_LIMIT)


def _resident(shape):
    nd = len(shape)
    return pl.BlockSpec(shape, lambda *_: (0,) * nd, pipeline_mode=pl.Buffered(1))


def _dot(a, b):
    return jnp.dot(a, b, preferred_element_type=F32)


def _dot_nt(a, b):
    return lax.dot_general(a, b, (((1,), (1,)), ((), ())), preferred_element_type=F32)


def _sigmoid(x):
    return 1.0 / (1.0 + jnp.exp(-x))


def _layer_norm(z, g, b):
    mu = jnp.mean(z, axis=-1, keepdims=True)
    zc = z - mu
    var = jnp.mean(zc * zc, axis=-1, keepdims=True)
    return zc * lax.rsqrt(var + LN_EPS) * g + b


def _modulate(x, mod_ref, j):
    shift = mod_ref[0, 3 * j:3 * j + 1, :]
    scale = mod_ref[0, 3 * j + 1:3 * j + 2, :]
    return (x * (1.0 + scale) + shift).astype(BF16)


def _rope(xs, c, s_up, s_dn, half):
    return (xs * c + pltpu.roll(xs, LANES - half, 1) * s_up + pltpu.roll(xs, half, 1) * s_dn)


def _ada_kernel(c_ref, w_ref, b_ref, o_ref):
    c = c_ref[...]
    cond = c * _sigmoid(c)
    o_ref[0] = jnp.dot(cond, w_ref[0], preferred_element_type=F32,
                       precision=lax.Precision.HIGHEST) + b_ref[0]


def _ada_call(c, w_ada, b_ada):
    depth, d, nd = w_ada.shape
    b = c.shape[0]
    n_blk = nd // d
    return pl.pallas_call(
        _ada_kernel,
        out_shape=jax.ShapeDtypeStruct((depth, b, nd), F32),
        grid=(depth, n_blk),
        in_specs=[pl.BlockSpec((b, d), lambda l, n: (0, 0)),
                  pl.BlockSpec((1, d, d), lambda l, n: (l, 0, n)),
                  pl.BlockSpec((1, 1, d), lambda l, n: (l, 0, n))],
        out_specs=pl.BlockSpec((1, b, d), lambda l, n: (l, 0, n)),
        compiler_params=_cparams(2),
        name="ada_mod",
    )(c, w_ada, b_ada.reshape(depth, 1, nd))


def _ffn_kernel(x_ref, mod_ref, win_ref, wout_ref, g_ref, b_ref, o_ref, *, j, dff, chunk, alpha):
    x = x_ref[0]
    h = _modulate(x, mod_ref, j)
    y = jnp.zeros(x.shape, F32)
    for c0 in range(0, dff, chunk):
        g = _dot(h, win_ref[:, c0:c0 + chunk])
        u = _dot(h, win_ref[:, dff + c0:dff + c0 + chunk])
        a = (g * _sigmoid(g) * u).astype(BF16)
        y = y + _dot(a, wout_ref[c0:c0 + chunk, :])
    gate = mod_ref[0, 3 * j + 2:3 * j + 3, :]
    z = alpha * x + 0.5 * (1.0 + gate) * y
    o_ref[0] = _layer_norm(z, g_ref[...], b_ref[...])


def _ffn_call(x, mod, w_in, w_out, g, b, *, j, alpha, tm):
    bsz, s, d = x.shape
    dff = w_out.shape[0]
    chunk = dff // 2 if (dff // 2) % LANES == 0 else dff
    kern = functools.partial(_ffn_kernel, j=j, dff=dff, chunk=chunk, alpha=alpha)
    return pl.pallas_call(
        kern,
        out_shape=jax.ShapeDtypeStruct(x.shape, F32),
        grid=(bsz, s // tm),
        in_specs=[pl.BlockSpec((1, tm, d), lambda bi, si: (bi, si, 0)),
                  pl.BlockSpec((1, N_ADA, d), lambda bi, si: (bi, 0, 0)),
                  _resident(w_in.shape), _resident(w_out.shape),
                  _resident((1, d)), _resident((1, d))],
        out_specs=pl.BlockSpec((1, tm, d), lambda bi, si: (bi, si, 0)),
        compiler_params=_cparams(2),
        name="ffn_sublayer",
    )(x, mod, w_in, w_out, g.reshape(1, d), b.reshape(1, d))


def _oproj_kernel(a_ref, x_ref, mod_ref, w_ref, g_ref, b_ref, o_ref, *, alpha):
    y = _dot(a_ref[0], w_ref[...])
    gate = mod_ref[0, 5:6, :]
    z = alpha * x_ref[0] + (1.0 + gate) * y
    o_ref[0] = _layer_norm(z, g_ref[...], b_ref[...])


def _oproj_call(a, x, mod, w, g, b, *, alpha, tm):
    bsz, s, d = x.shape
    return pl.pallas_call(
        functools.partial(_oproj_kernel, alpha=alpha),
        out_shape=jax.ShapeDtypeStruct(x.shape, F32),
        grid=(bsz, s // tm),
        in_specs=[pl.BlockSpec((1, tm, a.shape[-1]), lambda bi, si: (bi, si, 0)),
                  pl.BlockSpec((1, tm, d), lambda bi, si: (bi, si, 0)),
                  pl.BlockSpec((1, N_ADA, d), lambda bi, si: (bi, 0, 0)),
                  _resident(w.shape), _resident((1, d)), _resident((1, d))],
        out_specs=pl.BlockSpec((1, tm, d), lambda bi, si: (bi, si, 0)),
        compiler_params=_cparams(2),
        name="mixer_out_proj",
    )(a, x, mod, w, g.reshape(1, d), b.reshape(1, d))


def _rope_store(dst_ref, p, col0, n_blk, tabs, half, scale=None):
    c, s_up, s_dn = tabs
    for i in range(n_blk):
        xs = _rope(p[:, col0 + i * LANES:col0 + (i + 1) * LANES], c, s_up, s_dn, half)
        if scale is not None:
            xs = xs * scale
        dst_ref[0, :, i * LANES:(i + 1) * LANES] = xs.astype(dst_ref.dtype)


def _plain_store(dst_ref, p, col0, n_blk):
    dst_ref[0] = p[:, col0:col0 + n_blk * LANES].astype(dst_ref.dtype)


def _dsa_proj_kernel(x_ref, mod_ref, w_ref, kvn_ref, wkv_ref, c_ref, su_ref, sd_ref,
                     ci_ref, sui_ref, sdi_ref,
                     q_ref, k_ref, v_ref, qi_ref, ki_ref, wi_ref, *, d):
    h = _modulate(x_ref[0], mod_ref, 1)
    p = _dot(h, w_ref[...])
    tabs = (c_ref[...], su_ref[...], sd_ref[...])
    tabs_i = (ci_ref[...], sui_ref[...], sdi_ref[...])
    nq = d // LANES
    _rope_store(q_ref, p, 0, nq, tabs, ROT_DIM // 2, scale=HEAD_DIM ** -0.5)
    ckv = p[:, d:d + A_KV_RANK]
    ckv = ckv * lax.rsqrt(jnp.mean(ckv * ckv, axis=-1, keepdims=True) + RMS_EPS) * kvn_ref[...]
    kv = _dot(ckv.astype(BF16), wkv_ref[...])
    _rope_store(k_ref, kv, 0, 1, tabs, ROT_DIM // 2)
    _plain_store(v_ref, kv, LANES, 1)
    c0 = d + A_KV_RANK
    _rope_store(qi_ref, p, c0, 2, tabs_i, A_IDX_DIM // 8)
    _rope_store(ki_ref, p, c0 + 2 * LANES, 2, tabs_i, A_IDX_DIM // 8)
    wi_ref[0] = p[:, c0 + 4 * LANES:c0 + 5 * LANES] * (A_IDX_HEADS ** -0.5 * A_IDX_DIM ** -0.5)


def _fox_proj_kernel(x_ref, mod_ref, w_ref, fb_ref, q_ref, k_ref, v_ref, cum_ref, carry_ref, *, d, tm):
    h = _modulate(x_ref[0], mod_ref, 1)
    p = _dot(h, w_ref[...])
    q_ref[0] = (p[:, 0:d] * HEAD_DIM ** -0.5).astype(BF16)
    k_ref[0] = p[:, d:2 * d].astype(BF16)
    v_ref[0] = p[:, 2 * d:3 * d].astype(BF16)
    f = p[:, 3 * d:3 * d + LANES] + fb_ref[...]
    logf = jnp.minimum(f, 0.0) - jnp.log1p(jnp.exp(-jnp.abs(f)))

    @pl.when(pl.program_id(1) == 0)
    def _():
        carry_ref[...] = jnp.zeros_like(carry_ref)

    hi = logf.astype(BF16)
    r1 = logf - hi.astype(F32)
    mid = r1.astype(BF16)
    lo = (r1 - mid.astype(F32)).astype(BF16)
    row = lax.broadcasted_iota(jnp.int32, (tm, tm), 0)
    col = lax.broadcasted_iota(jnp.int32, (tm, tm), 1)
    tri = (row >= col).astype(BF16)
    cum = _dot(tri, hi) + _dot(tri, mid) + _dot(tri, lo) + carry_ref[...]
    cum_ref[0] = cum
    carry_ref[...] = cum[tm - 1:tm, :]


def _swa_proj_kernel(x_ref, mod_ref, w_ref, c_ref, su_ref, sd_ref, q_ref, k_ref, v_ref, *, d):
    h = _modulate(x_ref[0], mod_ref, 1)
    p = _dot(h, w_ref[...])
    tabs = (c_ref[...], su_ref[...], sd_ref[...])
    nq = d // LANES
    _rope_store(q_ref, p, 0, nq, tabs, ROT_DIM // 2, scale=HEAD_DIM ** -0.5)
    _rope_store(k_ref, p, d, 2, tabs, ROT_DIM // 2)
    _plain_store(v_ref, p, d + 2 * LANES, 2)


def _diff_proj_kernel(x_ref, mod_ref, w_ref, c_ref, su_ref, sd_ref, q_ref, k_ref, v_ref, *, d):
    h = _modulate(x_ref[0], mod_ref, 1)
    p = _dot(h, w_ref[...])
    tabs = (c_ref[...], su_ref[...], sd_ref[...])
    nq = d // LANES
    _rope_store(q_ref, p, 0, nq, tabs, ROT_DIM // 2, scale=HEAD_DIM ** -0.5)
    _rope_store(k_ref, p, d, nq, tabs, ROT_DIM // 2)
    _plain_store(v_ref, p, 2 * d, nq)


def _proj_call(kern, x, mod, consts, tabs, outs, *, tm, scratch=(), name):
    bsz, s, d = x.shape
    in_specs = [pl.BlockSpec((1, tm, d), lambda bi, si: (bi, si, 0)),
                pl.BlockSpec((1, N_ADA, d), lambda bi, si: (bi, 0, 0))]
    in_specs += [_resident(a.shape) for a in consts]
    in_specs += [pl.BlockSpec((tm, LANES), lambda bi, si: (si, 0)) for _ in tabs]
    return pl.pallas_call(
        kern,
        out_shape=[jax.ShapeDtypeStruct((bsz, s, w), dt) for w, dt in outs],
        grid=(bsz, s // tm),
        in_specs=in_specs,
        out_specs=[pl.BlockSpec((1, tm, w), lambda bi, si: (bi, si, 0)) for w, _ in outs],
        scratch_shapes=list(scratch),
        compiler_params=_cparams(2),
        name=name,
    )(x, mod, *consts, *tabs)


def _lane_is_low(shape):
    return lax.broadcasted_iota(jnp.int32, shape, 1) < HEAD_DIM


def _split_pair(q):
    low = _lane_is_low(q.shape)
    zero = jnp.zeros_like(q)
    return jnp.where(low, q, zero), jnp.where(low, zero, q)


def _online_update(s, m, l):
    m_new = jnp.maximum(m, jnp.max(s, axis=-1, keepdims=True))
    alpha = jnp.exp(m - m_new)
    p = jnp.exp(s - m_new)
    l_new = alpha * l + jnp.sum(p, axis=-1, keepdims=True)
    return p.astype(BF16), alpha, m_new, l_new


def _causal_mask(n):
    row = lax.broadcasted_iota(jnp.int32, (n, n), 0)
    col = lax.broadcasted_iota(jnp.int32, (n, n), 1)
    return row >= col


def _fox_attn_kernel(q_ref, k_ref, v_ref, cq_ref, ck_ref, o_ref, *, t):
    i = pl.program_id(2)
    q0, q1 = _split_pair(q_ref[0])
    cq = cq_ref[0, 0]
    cq0, cq1 = cq[:, 0:1], cq[:, 1:2]
    low = _lane_is_low((t, LANES))

    def step(j, carry, masked):
        m0, l0, m1, l1, acc = carry
        kj = k_ref[0, pl.ds(pl.multiple_of(j * t, t), t), :]
        vj = v_ref[0, pl.ds(pl.multiple_of(j * t, t), t), :]
        ck = ck_ref[0, 0, j]
        s0 = _dot_nt(q0, kj) + (cq0 - ck[0:1, :])
        s1 = _dot_nt(q1, kj) + (cq1 - ck[1:2, :])
        if masked:
            ok = _causal_mask(t)
            s0 = jnp.where(ok, s0, NEG)
            s1 = jnp.where(ok, s1, NEG)
        p0, a0, m0, l0 = _online_update(s0, m0, l0)
        p1, a1, m1, l1 = _online_update(s1, m1, l1)
        acc = jnp.where(low, a0, a1) * acc + jnp.where(low, _dot(p0, vj), _dot(p1, vj))
        return m0, l0, m1, l1, acc

    col1 = jnp.full((t, 1), NEG, F32)
    zero1 = jnp.zeros((t, 1), F32)
    init = (col1, zero1, col1, zero1, jnp.zeros((t, LANES), F32))
    carry = lax.fori_loop(0, i, lambda j, c: step(j, c, False), init)
    _, l0, _, l1, acc = step(i, carry, True)
    o_ref[0] = (acc / jnp.where(low, l0, l1)).astype(o_ref.dtype)


def _diff_attn_kernel(q_ref, k_ref, v_ref, lam_ref, g_ref, o_ref, *, t, lambda_init):
    i = pl.program_id(2)
    q0, q1 = _split_pair(q_ref[0])

    def step(j, carry, masked):
        m0, l0, acc0, m1, l1, acc1 = carry
        kj = k_ref[0, pl.ds(pl.multiple_of(j * t, t), t), :]
        vj = v_ref[0, pl.ds(pl.multiple_of(j * t, t), t), :]
        s0 = _dot_nt(q0, kj)
        s1 = _dot_nt(q1, kj)
        if masked:
            ok = _causal_mask(t)
            s0 = jnp.where(ok, s0, NEG)
            s1 = jnp.where(ok, s1, NEG)
        p0, a0, m0, l0 = _online_update(s0, m0, l0)
        p1, a1, m1, l1 = _online_update(s1, m1, l1)
        acc0 = a0 * acc0 + _dot(p0, vj)
        acc1 = a1 * acc1 + _dot(p1, vj)
        return m0, l0, acc0, m1, l1, acc1

    col1 = jnp.full((t, 1), NEG, F32)
    zero1 = jnp.zeros((t, 1), F32)
    zacc = jnp.zeros((t, LANES), F32)
    carry = lax.fori_loop(0, i, lambda j, c: step(j, c, False), (col1, zero1, zacc, col1, zero1, zacc))
    _, l0, acc0, _, l1, acc1 = step(i, carry, True)
    lam = lam_ref[...]
    lam_val = (jnp.exp(jnp.sum(lam[0:1] * lam[1:2], axis=-1, keepdims=True))
               - jnp.exp(jnp.sum(lam[2:3] * lam[3:4], axis=-1, keepdims=True)) + lambda_init)
    out = acc0 / l0 - lam_val * (acc1 / l1)
    out = out * lax.rsqrt(jnp.mean(out * out, axis=-1, keepdims=True) + RMS_EPS) * g_ref[...]
    o_ref[0] = (out * (1.0 - lambda_init)).astype(o_ref.dtype)


def _swa_attn_kernel(sink_ref, q_ref, k_ref, v_ref, o_ref, *, t, blk):
    hp = pl.program_id(1)
    i = pl.program_id(2)
    low = _lane_is_low((blk, LANES))
    sink0 = sink_ref[2 * hp]
    sink1 = sink_ref[2 * hp + 1]
    row = lax.broadcasted_iota(jnp.int32, (blk, 2 * blk), 0)
    col = lax.broadcasted_iota(jnp.int32, (blk, 2 * blk), 1)
    for u in range(t // blk):
        r0 = i * t + u * blk
        k0 = pl.multiple_of(jnp.maximum(r0 - blk, 0), blk)
        q0, q1 = _split_pair(q_ref[0, u * blk:(u + 1) * blk, :])
        kj = k_ref[0, pl.ds(k0, 2 * blk), :]
        vj = v_ref[0, pl.ds(k0, 2 * blk), :]
        dist = (r0 + row) - (k0 + col)
        ok = (dist >= 0) & (dist < C_WINDOW)
        outs = []
        for qe, sink in ((q0, sink0), (q1, sink1)):
            s = jnp.where(ok, _dot_nt(qe, kj), NEG)
            m = jnp.maximum(jnp.max(s, axis=-1, keepdims=True), sink)
            p = jnp.exp(s - m)
            denom = jnp.sum(p, axis=-1, keepdims=True) + jnp.exp(sink - m)
            outs.append(_dot(p.astype(BF16), vj) / denom)
        o_ref[0, u * blk:(u + 1) * blk, :] = jnp.where(low, outs[0], outs[1]).astype(o_ref.dtype)


def _sortable(score):
    bits = lax.bitcast_convert_type(score, jnp.int32)
    return bits ^ (lax.shift_right_arithmetic(bits, 31) & jnp.int32(0x7FFFFFFF))


def _lane_fold(x):
    acc = x[:, 0:LANES]
    for c0 in range(LANES, x.shape[1], LANES):
        acc = acc + x[:, c0:c0 + LANES]
    return acc


def _dsa_attn_kernel(q_ref, k_ref, v_ref, qi_ref, ki_ref, wi_ref, o_ref, keys_ref, selb_ref, *, t, k_sel):
    i = pl.program_id(1)
    hp = pl.program_id(2)

    def count_rows(pred_fn):
        def body(j, acc):
            return acc + _lane_fold(jnp.where(pred_fn(keys_ref[j]), 1.0, 0.0))
        acc = lax.fori_loop(0, i + 1, body, jnp.zeros((t, LANES), F32))
        return jnp.sum(acc, axis=-1, keepdims=True)

    @pl.when(hp == 0)
    def _select():
        qi = qi_ref[0]
        lane = lax.broadcasted_iota(jnp.int32, qi.shape, 1)
        qis = [jnp.where((lane >= h * A_IDX_DIM) & (lane < (h + 1) * A_IDX_DIM), qi, jnp.zeros_like(qi))
               for h in range(A_IDX_HEADS)]
        wi = wi_ref[0]
        ws = [wi[:, h:h + 1] for h in range(A_IDX_HEADS)]

        def score_tile(j, masked):
            kij = ki_ref[0, pl.ds(pl.multiple_of(j * t, t), t), :]
            score = jnp.zeros((t, t), F32)
            for h in range(A_IDX_HEADS):
                score = score + ws[h] * jnp.maximum(_dot_nt(qis[h], kij), 0.0)
            key = _sortable(score)
            if masked:
                key = jnp.where(_causal_mask(t), key, INT_MIN)
            keys_ref[j] = key

        def score_body(j, c):
            score_tile(j, False)
            return c
        lax.fori_loop(0, i, score_body, 0)
        score_tile(i, True)

        def bit_body(b, thr_u):
            cand = thr_u | lax.shift_left(jnp.int32(1), 31 - b)
            cand_s = cand ^ INT_MIN
            cnt = count_rows(lambda key: key >= cand_s)
            return jnp.where(cnt >= k_sel, cand, thr_u)
        thr_u = lax.fori_loop(0, 32, bit_body, jnp.zeros((t, 1), jnp.int32))
        thr = thr_u ^ INT_MIN
        need = k_sel - count_rows(lambda key: key > thr)

        a_idx = lax.broadcasted_iota(jnp.int32, (t, t), 0)
        b_idx = lax.broadcasted_iota(jnp.int32, (t, t), 1)
        before = (a_idx < b_idx).astype(BF16)

        def sel_tile(j, seen, masked):
            key = keys_ref[j]
            eq = key == thr
            eq_f = jnp.where(eq, 1.0, 0.0)
            rank = _dot(eq_f.astype(BF16), before) + seen
            sel = (key > thr) | (eq & (rank < need))
            if masked:
                sel = sel & _causal_mask(t)
            selb_ref[j] = jnp.where(sel, 0.0, NEG)
            return seen + jnp.sum(_lane_fold(eq_f), axis=-1, keepdims=True)

        seen = lax.fori_loop(0, i, lambda j, c: sel_tile(j, c, False), jnp.zeros((t, 1), F32))
        sel_tile(i, seen, True)

    q0, q1 = _split_pair(q_ref[0])
    low = _lane_is_low((t, LANES))

    def step(j, carry):
        m0, l0, m1, l1, acc = carry
        kj = k_ref[0, pl.ds(pl.multiple_of(j * t, t), t), :]
        vj = v_ref[0, pl.ds(pl.multiple_of(j * t, t), t), :]
        bias = selb_ref[j]
        p0, a0, m0, l0 = _online_update(_dot_nt(q0, kj) + bias, m0, l0)
        p1, a1, m1, l1 = _online_update(_dot_nt(q1, kj) + bias, m1, l1)
        acc = jnp.where(low, a0, a1) * acc + jnp.where(low, _dot(p0, vj), _dot(p1, vj))
        return m0, l0, m1, l1, acc

    col1 = jnp.full((t, 1), NEG, F32)
    zero1 = jnp.zeros((t, 1), F32)
    _, l0, _, l1, acc = lax.fori_loop(0, i + 1, step, (col1, zero1, col1, zero1, jnp.zeros((t, LANES), F32)))
    o_ref[0] = (acc / jnp.where(low, l0, l1)).astype(o_ref.dtype)


def _rope_lane_tables(seq, rot_dim, period):
    half = rot_dim // 2
    inv = ROPE_THETA ** (-jnp.arange(0, rot_dim, 2, dtype=F32) / rot_dim)
    ang = jnp.arange(seq, dtype=F32)[:, None] * inv[None, :]
    cos, sin = jnp.cos(ang), jnp.sin(ang)
    lane = jnp.arange(LANES) % period
    idx = jnp.where(lane < half, lane, jnp.clip(lane - half, 0, half - 1))
    first = (lane < half)[None, :]
    second = ((lane >= half) & (lane < 2 * half))[None, :]
    c = jnp.where(first | second, cos[:, idx], 1.0)
    s_up = jnp.where(first, -sin[:, idx], 0.0)
    s_dn = jnp.where(second, sin[:, idx], 0.0)
    return c.astype(F32), s_up.astype(F32), s_dn.astype(F32)


def _pad_cols(w, n):
    return jnp.pad(w, ((0, 0), (0, n - w.shape[1])))


def _attn_tile(s):
    return min(512, s)


def _mixer_dsa(x, mod, w_in, kv_norm, w_kv_up, tabs, tabs_i, *, tm):
    bsz, s, d = x.shape
    t = _attn_tile(s)
    k_sel = min(A_TOPK_MAX, s // 4)
    n_qi = A_IDX_HEADS * A_IDX_DIM
    c0 = d + A_KV_RANK
    w_ki = w_in[:, c0 + n_qi:c0 + n_qi + A_IDX_DIM]
    w_wi = w_in[:, c0 + n_qi + A_IDX_DIM:]
    w_cat = jnp.concatenate([w_in[:, :c0 + n_qi], jnp.tile(w_ki, (1, n_qi // A_IDX_DIM)),
                             _pad_cols(w_wi, LANES)], axis=1).astype(BF16)
    wk, wv = w_kv_up[:, :HEAD_DIM], w_kv_up[:, HEAD_DIM:]
    wkv = jnp.concatenate([wk, wk, wv, wv], axis=1).astype(BF16)
    q, k2, v2, qi, ki, wi = _proj_call(
        functools.partial(_dsa_proj_kernel, d=d), x, mod,
        [w_cat, kv_norm.reshape(1, -1), wkv], list(tabs) + list(tabs_i),
        [(d, BF16), (LANES, BF16), (LANES, BF16), (n_qi, BF16), (n_qi, BF16), (LANES, F32)],
        tm=tm, name="dsa_in_proj")
    nt = s // t
    return pl.pallas_call(
        functools.partial(_dsa_attn_kernel, t=t, k_sel=k_sel),
        out_shape=jax.ShapeDtypeStruct((bsz, s, d), BF16),
        grid=(bsz, nt, d // LANES),
        in_specs=[pl.BlockSpec((1, t, LANES), lambda b, i, h: (b, i, h)),
                  pl.BlockSpec((1, s, LANES), lambda b, i, h: (b, 0, 0)),
                  pl.BlockSpec((1, s, LANES), lambda b, i, h: (b, 0, 0)),
                  pl.BlockSpec((1, t, n_qi), lambda b, i, h: (b, i, 0)),
                  pl.BlockSpec((1, s, n_qi), lambda b, i, h: (b, 0, 0)),
                  pl.BlockSpec((1, t, LANES), lambda b, i, h: (b, i, 0))],
        out_specs=pl.BlockSpec((1, t, LANES), lambda b, i, h: (b, i, h)),
        scratch_shapes=[pltpu.VMEM((nt, t, t), jnp.int32), pltpu.VMEM((nt, t, t), F32)],
        compiler_params=_cparams(3),
        name="dsa_attention",
    )(q, k2, v2, qi, ki, wi)


def _mixer_fox(x, mod, w_in, f_bias, *, tm):
    bsz, s, d = x.shape
    t = _attn_tile(s)
    nh = d // HEAD_DIM
    w_cat = _pad_cols(w_in, 3 * d + LANES).astype(BF16)
    fb = _pad_cols(f_bias.reshape(1, -1), LANES)
    q, k, v, cum = _proj_call(
        functools.partial(_fox_proj_kernel, d=d, tm=tm), x, mod, [w_cat, fb], [],
        [(d, BF16), (d, BF16), (d, BF16), (LANES, F32)],
        tm=tm, scratch=[pltpu.VMEM((1, LANES), F32)], name="fox_in_proj")
    nt = s // t
    cum = cum[:, :, :nh].reshape(bsz, s, nh // 2, 2)
    cq = jnp.transpose(cum, (0, 2, 1, 3))
    ck = jnp.transpose(cum.reshape(bsz, nt, t, nh // 2, 2), (0, 3, 1, 4, 2))
    return pl.pallas_call(
        functools.partial(_fox_attn_kernel, t=t),
        out_shape=jax.ShapeDtypeStruct((bsz, s, d), BF16),
        grid=(bsz, d // LANES, nt),
        in_specs=[pl.BlockSpec((1, t, LANES), lambda b, h, i: (b, i, h)),
                  pl.BlockSpec((1, s, LANES), lambda b, h, i: (b, 0, h)),
                  pl.BlockSpec((1, s, LANES), lambda b, h, i: (b, 0, h)),
                  pl.BlockSpec((1, 1, t, 2), lambda b, h, i: (b, h, i, 0)),
                  pl.BlockSpec((1, 1, nt, 2, t), lambda b, h, i: (b, h, 0, 0, 0))],
        out_specs=pl.BlockSpec((1, t, LANES), lambda b, h, i: (b, i, h)),
        compiler_params=_cparams(3),
        name="fox_attention",
    )(q, k, v, cq, ck)


def _mixer_swa(x, mod, w_in, sinks, tabs, *, tm):
    bsz, s, d = x.shape
    t = _attn_tile(s)
    n_kv = (w_in.shape[1] - d) // (2 * HEAD_DIM)
    pairs_per_kv = (d // LANES) // n_kv
    wq, wk, wv = w_in[:, :d], w_in[:, d:d + n_kv * HEAD_DIM], w_in[:, d + n_kv * HEAD_DIM:]

    def dup(w):
        w = w.reshape(w.shape[0], n_kv, 1, HEAD_DIM)
        return jnp.broadcast_to(w, (w.shape[0], n_kv, 2, HEAD_DIM)).reshape(w.shape[0], n_kv * LANES)

    w_cat = jnp.concatenate([wq, dup(wk), dup(wv)], axis=1).astype(BF16)
    q, k2, v2 = _proj_call(
        functools.partial(_swa_proj_kernel, d=d), x, mod, [w_cat], list(tabs),
        [(d, BF16), (n_kv * LANES, BF16), (n_kv * LANES, BF16)], tm=tm, name="swa_in_proj")
    return pl.pallas_call(
        functools.partial(_swa_attn_kernel, t=t, blk=C_WINDOW),
        out_shape=jax.ShapeDtypeStruct((bsz, s, d), BF16),
        grid=(bsz, d // LANES, s // t),
        in_specs=[pl.BlockSpec(memory_space=pltpu.SMEM),
                  pl.BlockSpec((1, t, LANES), lambda b, h, i: (b, i, h)),
                  pl.BlockSpec((1, s, LANES), lambda b, h, i: (b, 0, h // pairs_per_kv)),
                  pl.BlockSpec((1, s, LANES), lambda b, h, i: (b, 0, h // pairs_per_kv))],
        out_specs=pl.BlockSpec((1, t, LANES), lambda b, h, i: (b, i, h)),
        compiler_params=_cparams(3),
        name="swa_attention",
    )(sinks.astype(F32), q, k2, v2)


def _mixer_diff(x, mod, w_in, lam, subln_g, tabs, lambda_init, *, tm):
    bsz, s, d = x.shape
    t = _attn_tile(s)
    q, k, v = _proj_call(
        functools.partial(_diff_proj_kernel, d=d), x, mod, [w_in.astype(BF16)], list(tabs),
        [(d, BF16), (d, BF16), (d, BF16)], tm=tm, name="diff_in_proj")
    return pl.pallas_call(
        functools.partial(_diff_attn_kernel, t=t, lambda_init=lambda_init),
        out_shape=jax.ShapeDtypeStruct((bsz, s, d), BF16),
        grid=(bsz, d // LANES, s // t),
        in_specs=[pl.BlockSpec((1, t, LANES), lambda b, h, i: (b, i, h)),
                  pl.BlockSpec((1, s, LANES), lambda b, h, i: (b, 0, h)),
                  pl.BlockSpec((1, s, LANES), lambda b, h, i: (b, 0, h)),
                  pl.BlockSpec(lam.shape, lambda b, h, i: (0, 0)),
                  pl.BlockSpec((1, LANES), lambda b, h, i: (0, 0))],
        out_specs=pl.BlockSpec((1, t, LANES), lambda b, h, i: (b, i, h)),
        compiler_params=_cparams(3),
        name="diff_attention",
    )(q, k, v, lam.astype(F32), subln_g.reshape(1, -1))


def kernel(x, c, ln_g, ln_b, w_ada, b_ada, w_ffn_in, w_ffn_out, dsa_w_in, dsa_kv_norm, dsa_w_kv_up, dsa_w_out, fox_w_in, fox_f_bias, fox_w_out, swa_w_in, swa_sinks, swa_w_out, diff_w_in, diff_lambda, diff_subln, diff_w_out):
    bsz, s, d = x.shape
    depth = w_ada.shape[0]
    n_mixers = 4
    alpha = (2 * depth) ** 0.25
    tm = min(512, s)
    tabs = _rope_lane_tables(s, ROT_DIM, HEAD_DIM)
    tabs_i = _rope_lane_tables(s, A_IDX_DIM // 4, A_IDX_DIM)
    mods = _ada_call(c, w_ada, b_ada).reshape(depth, bsz, N_ADA, d)
    for i in range(depth):
        m, r = i % n_mixers, i // n_mixers
        mod = mods[i]
        x = _ffn_call(x, mod, w_ffn_in[i, 0].astype(BF16), w_ffn_out[i, 0].astype(BF16),
                      ln_g[i, 0], ln_b[i, 0], j=0, alpha=alpha, tm=tm)
        if m == 0:
            a = _mixer_dsa(x, mod, dsa_w_in[r], dsa_kv_norm[r], dsa_w_kv_up[r], tabs, tabs_i, tm=tm)
            w_out = dsa_w_out[r]
        elif m == 1:
            a = _mixer_fox(x, mod, fox_w_in[r], fox_f_bias[r], tm=tm)
            w_out = fox_w_out[r]
        elif m == 2:
            a = _mixer_swa(x, mod, swa_w_in[r], swa_sinks[r], tabs, tm=tm)
            w_out = swa_w_out[r]
        else:
            lambda_init = 0.8 - 0.6 * math.exp(-0.3 * i)
            a = _mixer_diff(x, mod, diff_w_in[r], diff_lambda[r], diff_subln[r], tabs, lambda_init, tm=tm)
            w_out = diff_w_out[r]
        x = _oproj_call(a, x, mod, w_out.astype(BF16), ln_g[i, 1], ln_b[i, 1], alpha=alpha, tm=tm)
        x = _ffn_call(x, mod, w_ffn_in[i, 1].astype(BF16), w_ffn_out[i, 1].astype(BF16),
                      ln_g[i, 2], ln_b[i, 2], j=2, alpha=alpha, tm=tm)
    return x
```

```python
import functools
import math

import jax
import jax.numpy as jnp
from jax import lax
from jax.experimental import pallas as pl
from jax.experimental.pallas import tpu as pltpu

F32 = jnp.float32
BF16 = jnp.bfloat16

HEAD_DIM = 64
ROT_DIM = HEAD_DIM // 4
ROPE_THETA = 500000.0
LN_EPS = 1e-5
RMS_EPS = 1e-6
N_ADA = 9
A_KV_RANK = 128
A_IDX_HEADS = 8
A_IDX_DIM = 32
A_TOPK_MAX = 256
C_WINDOW = 128
LANES = 128
NEG = -1e30
INT_MIN = -(2 ** 31)
LOG2E = math.log2(math.e)
Q_SCALE = HEAD_DIM ** -0.5 * LOG2E
VMEM_LIMIT = 56 * 1024 * 1024


def _cparams(n_axes):
    return pltpu.CompilerParams(dimension_semantics=("arbitrary",) * n_axes,
                                vmem_limit_bytes=VMEM_LIMIT)


def _resident(shape):
    nd = len(shape)
    return pl.BlockSpec(shape, lambda *_: (0,) * nd, pipeline_mode=pl.Buffered(1))


def _dot(a, b):
    return jnp.dot(a, b, preferred_element_type=F32)


def _dot_nt(a, b):
    return lax.dot_general(a, b, (((1,), (1,)), ((), ())), preferred_element_type=F32)


def _sigmoid(x):
    return 1.0 / (1.0 + jnp.exp(-x))


def _layer_norm(z, g, b):
    mu = jnp.mean(z, axis=-1, keepdims=True)
    zc = z - mu
    var = jnp.mean(zc * zc, axis=-1, keepdims=True)
    return zc * lax.rsqrt(var + LN_EPS) * g + b


def _modulate(x, mod_ref, j):
    shift = mod_ref[0, 3 * j:3 * j + 1, :]
    scale = mod_ref[0, 3 * j + 1:3 * j + 2, :]
    return (x * (1.0 + scale) + shift).astype(BF16)


def _rope(xs, c, s_up, s_dn, half):
    return (xs * c + pltpu.roll(xs, LANES - half, 1) * s_up + pltpu.roll(xs, half, 1) * s_dn)


def _ada_kernel(c_ref, w_ref, b_ref, o_ref):
    c = c_ref[...]
    cond = c * _sigmoid(c)
    o_ref[0] = jnp.dot(cond, w_ref[0], preferred_element_type=F32,
                       precision=lax.Precision.HIGHEST) + b_ref[0]


def _ada_call(c, w_ada, b_ada):
    depth, d, nd = w_ada.shape
    b = c.shape[0]
    n_blk = nd // d
    return pl.pallas_call(
        _ada_kernel,
        out_shape=jax.ShapeDtypeStruct((depth, b, nd), F32),
        grid=(depth, n_blk),
        in_specs=[pl.BlockSpec((b, d), lambda l, n: (0, 0)),
                  pl.BlockSpec((1, d, d), lambda l, n: (l, 0, n)),
                  pl.BlockSpec((1, 1, d), lambda l, n: (l, 0, n))],
        out_specs=pl.BlockSpec((1, b, d), lambda l, n: (l, 0, n)),
        compiler_params=_cparams(2),
        name="ada_mod",
    )(c, w_ada, b_ada.reshape(depth, 1, nd))


def _ffn_kernel(x_ref, mod_ref, win_ref, wout_ref, g_ref, b_ref, o_ref, *, j, dff, chunk, alpha):
    x = x_ref[0]
    h = _modulate(x, mod_ref, j)
    y = jnp.zeros(x.shape, F32)
    for c0 in range(0, dff, chunk):
        g = _dot(h, win_ref[:, c0:c0 + chunk])
        u = _dot(h, win_ref[:, dff + c0:dff + c0 + chunk])
        a = (g * _sigmoid(g) * u).astype(BF16)
        y = y + _dot(a, wout_ref[c0:c0 + chunk, :])
    gate = mod_ref[0, 3 * j + 2:3 * j + 3, :]
    z = alpha * x + 0.5 * (1.0 + gate) * y
    o_ref[0] = _layer_norm(z, g_ref[...], b_ref[...])


def _ffn_call(x, mod, w_in, w_out, g, b, *, j, alpha, tm):
    bsz, s, d = x.shape
    dff = w_out.shape[0]
    chunk = dff // 2 if (dff // 2) % LANES == 0 else dff
    kern = functools.partial(_ffn_kernel, j=j, dff=dff, chunk=chunk, alpha=alpha)
    return pl.pallas_call(
        kern,
        out_shape=jax.ShapeDtypeStruct(x.shape, F32),
        grid=(bsz, s // tm),
        in_specs=[pl.BlockSpec((1, tm, d), lambda bi, si: (bi, si, 0)),
                  pl.BlockSpec((1, N_ADA, d), lambda bi, si: (bi, 0, 0)),
                  _resident(w_in.shape), _resident(w_out.shape),
                  _resident((1, d)), _resident((1, d))],
        out_specs=pl.BlockSpec((1, tm, d), lambda bi, si: (bi, si, 0)),
        compiler_params=_cparams(2),
        name="ffn_sublayer",
    )(x, mod, w_in, w_out, g.reshape(1, d), b.reshape(1, d))


def _oproj_kernel(a_ref, x_ref, mod_ref, w_ref, g_ref, b_ref, o_ref, *, alpha):
    y = _dot(a_ref[0], w_ref[...])
    gate = mod_ref[0, 5:6, :]
    z = alpha * x_ref[0] + (1.0 + gate) * y
    o_ref[0] = _layer_norm(z, g_ref[...], b_ref[...])


def _oproj_call(a, x, mod, w, g, b, *, alpha, tm):
    bsz, s, d = x.shape
    return pl.pallas_call(
        functools.partial(_oproj_kernel, alpha=alpha),
        out_shape=jax.ShapeDtypeStruct(x.shape, F32),
        grid=(bsz, s // tm),
        in_specs=[pl.BlockSpec((1, tm, a.shape[-1]), lambda bi, si: (bi, si, 0)),
                  pl.BlockSpec((1, tm, d), lambda bi, si: (bi, si, 0)),
                  pl.BlockSpec((1, N_ADA, d), lambda bi, si: (bi, 0, 0)),
                  _resident(w.shape), _resident((1, d)), _resident((1, d))],
        out_specs=pl.BlockSpec((1, tm, d), lambda bi, si: (bi, si, 0)),
        compiler_params=_cparams(2),
        name="mixer_out_proj",
    )(a, x, mod, w, g.reshape(1, d), b.reshape(1, d))


def _rope_store(dst_ref, p, col0, n_blk, tabs, half, scale=None):
    c, s_up, s_dn = tabs
    for i in range(n_blk):
        xs = _rope(p[:, col0 + i * LANES:col0 + (i + 1) * LANES], c, s_up, s_dn, half)
        if scale is not None:
            xs = xs * scale
        dst_ref[0, :, i * LANES:(i + 1) * LANES] = xs.astype(dst_ref.dtype)


def _plain_store(dst_ref, p, col0, n_blk):
    dst_ref[0] = p[:, col0:col0 + n_blk * LANES].astype(dst_ref.dtype)


def _dsa_proj_kernel(x_ref, mod_ref, w_ref, kvn_ref, wkv_ref, c_ref, su_ref, sd_ref,
                     ci_ref, sui_ref, sdi_ref,
                     q_ref, k_ref, v_ref, qi_ref, ki_ref, wi_ref, *, d):
    h = _modulate(x_ref[0], mod_ref, 1)
    p = _dot(h, w_ref[...])
    tabs = (c_ref[...], su_ref[...], sd_ref[...])
    tabs_i = (ci_ref[...], sui_ref[...], sdi_ref[...])
    nq = d // LANES
    _rope_store(q_ref, p, 0, nq, tabs, ROT_DIM // 2, scale=Q_SCALE)
    ckv = p[:, d:d + A_KV_RANK]
    ckv = ckv * lax.rsqrt(jnp.mean(ckv * ckv, axis=-1, keepdims=True) + RMS_EPS) * kvn_ref[...]
    kv = _dot(ckv.astype(BF16), wkv_ref[...])
    _rope_store(k_ref, kv, 0, 1, tabs, ROT_DIM // 2)
    _plain_store(v_ref, kv, LANES, 1)
    c0 = d + A_KV_RANK
    _rope_store(qi_ref, p, c0, 2, tabs_i, A_IDX_DIM // 8)
    _rope_store(ki_ref, p, c0 + 2 * LANES, 2, tabs_i, A_IDX_DIM // 8)
    wi_ref[0] = p[:, c0 + 4 * LANES:c0 + 5 * LANES] * (A_IDX_HEADS ** -0.5 * A_IDX_DIM ** -0.5)


def _fox_proj_kernel(x_ref, mod_ref, w_ref, fb_ref, q_ref, k_ref, v_ref, cum_ref, carry_ref, *, d, tm):
    h = _modulate(x_ref[0], mod_ref, 1)
    p = _dot(h, w_ref[...])
    q_ref[0] = (p[:, 0:d] * Q_SCALE).astype(BF16)
    k_ref[0] = p[:, d:2 * d].astype(BF16)
    v_ref[0] = p[:, 2 * d:3 * d].astype(BF16)
    f = p[:, 3 * d:3 * d + LANES] + fb_ref[...]
    logf = jnp.minimum(f, 0.0) - jnp.log1p(jnp.exp(-jnp.abs(f)))

    @pl.when(pl.program_id(1) == 0)
    def _():
        carry_ref[...] = jnp.zeros_like(carry_ref)

    hi = logf.astype(BF16)
    r1 = logf - hi.astype(F32)
    mid = r1.astype(BF16)
    lo = (r1 - mid.astype(F32)).astype(BF16)
    row = lax.broadcasted_iota(jnp.int32, (tm, tm), 0)
    col = lax.broadcasted_iota(jnp.int32, (tm, tm), 1)
    tri = (row >= col).astype(BF16)
    cum = _dot(tri, hi) + _dot(tri, mid) + _dot(tri, lo) + carry_ref[...]
    cum_ref[0] = cum * LOG2E
    carry_ref[...] = cum[tm - 1:tm, :]


def _swa_proj_kernel(x_ref, mod_ref, w_ref, c_ref, su_ref, sd_ref, q_ref, k_ref, v_ref, *, d):
    h = _modulate(x_ref[0], mod_ref, 1)
    p = _dot(h, w_ref[...])
    tabs = (c_ref[...], su_ref[...], sd_ref[...])
    nq = d // LANES
    _rope_store(q_ref, p, 0, nq, tabs, ROT_DIM // 2, scale=Q_SCALE)
    _rope_store(k_ref, p, d, 2, tabs, ROT_DIM // 2)
    _plain_store(v_ref, p, d + 2 * LANES, 2)


def _diff_proj_kernel(x_ref, mod_ref, w_ref, c_ref, su_ref, sd_ref, q_ref, k_ref, v_ref, *, d):
    h = _modulate(x_ref[0], mod_ref, 1)
    p = _dot(h, w_ref[...])
    tabs = (c_ref[...], su_ref[...], sd_ref[...])
    nq = d // LANES
    _rope_store(q_ref, p, 0, nq, tabs, ROT_DIM // 2, scale=Q_SCALE)
    _rope_store(k_ref, p, d, nq, tabs, ROT_DIM // 2)
    _plain_store(v_ref, p, 2 * d, nq)


def _proj_call(kern, x, mod, consts, tabs, outs, *, tm, scratch=(), name):
    bsz, s, d = x.shape
    in_specs = [pl.BlockSpec((1, tm, d), lambda bi, si: (bi, si, 0)),
                pl.BlockSpec((1, N_ADA, d), lambda bi, si: (bi, 0, 0))]
    in_specs += [_resident(a.shape) for a in consts]
    in_specs += [pl.BlockSpec((tm, LANES), lambda bi, si: (si, 0)) for _ in tabs]
    return pl.pallas_call(
        kern,
        out_shape=[jax.ShapeDtypeStruct((bsz, s, w), dt) for w, dt in outs],
        grid=(bsz, s // tm),
        in_specs=in_specs,
        out_specs=[pl.BlockSpec((1, tm, w), lambda bi, si: (bi, si, 0)) for w, _ in outs],
        scratch_shapes=list(scratch),
        compiler_params=_cparams(2),
        name=name,
    )(x, mod, *consts, *tabs)


def _lane_is_low(shape):
    return lax.broadcasted_iota(jnp.int32, shape, 1) < HEAD_DIM


def _split_pair(q):
    low = _lane_is_low(q.shape)
    zero = jnp.zeros_like(q)
    return jnp.where(low, q, zero), jnp.where(low, zero, q)


def _lane_tile(x, width):
    return jnp.concatenate([x] * (width // LANES), axis=1)


def _softmax_tile(s, m, row_term=None):
    tile_max = jnp.broadcast_to(jnp.max(s, axis=-1, keepdims=True), m.shape)
    if row_term is not None:
        tile_max = tile_max + row_term
    m_new = jnp.maximum(m, tile_max)
    shift = m_new if row_term is None else m_new - row_term
    p = jnp.exp2(s - _lane_tile(shift, s.shape[1])).astype(BF16)
    return p, jnp.exp2(m - m_new), m_new


def _causal_mask(n):
    row = lax.broadcasted_iota(jnp.int32, (n, n), 0)
    col = lax.broadcasted_iota(jnp.int32, (n, n), 1)
    return row >= col


def _flash_tiles(n_plain, logits, consume, state, last_masked=True):
    def block2(ja, jb, masked_b, st):
        sa = logits(ja, False)
        sb = logits(jb, masked_b)
        return consume(jb, sb, consume(ja, sa, st))

    state = lax.fori_loop(0, n_plain // 2, lambda j, st: block2(2 * j, 2 * j + 1, False, st), state)
    return lax.cond(n_plain % 2 == 1,
                    lambda st: block2(n_plain - 1, n_plain, last_masked, st),
                    lambda st: consume(n_plain, logits(n_plain, last_masked), st),
                    state)


def _pair_values(vj):
    low = _lane_is_low(vj.shape)
    one = jnp.ones_like(vj)
    return jnp.where(low, vj, one), jnp.where(low, one, vj)


def _pair_init(t):
    m = jnp.full((t, LANES), NEG, F32)
    zacc = jnp.zeros((t, LANES), F32)
    return m, m, zacc, zacc


def _pair_consume(s, state, v0, v1, row_terms=(None, None)):
    m0, m1, acc0, acc1 = state
    p0, a0, m0 = _softmax_tile(s[0], m0, row_terms[0])
    p1, a1, m1 = _softmax_tile(s[1], m1, row_terms[1])
    return m0, m1, a0 * acc0 + _dot(p0, v0), a1 * acc1 + _dot(p1, v1)


def _pair_finish(acc0, acc1):
    low = _lane_is_low(acc0.shape)
    return jnp.where(low, acc0 / pltpu.roll(acc0, HEAD_DIM, 1), acc1 / pltpu.roll(acc1, HEAD_DIM, 1))


def _key_tile(ref, j, t):
    return ref[0, pl.ds(pl.multiple_of(j * t, t), t), :]


def _fox_attn_kernel(q_ref, k_ref, v_ref, cq_ref, ck_ref, o_ref, *, t):
    i = pl.program_id(2)
    q0, q1 = _split_pair(q_ref[0])
    cq = cq_ref[0, 0]
    cq0 = jnp.broadcast_to(cq[:, 0:1], (t, LANES))
    cq1 = jnp.broadcast_to(cq[:, 1:2], (t, LANES))

    def logits(j, masked):
        kj = _key_tile(k_ref, j, t)
        ck = ck_ref[0, 0, j]
        s0 = _dot_nt(q0, kj) - ck[0:1, :]
        s1 = _dot_nt(q1, kj) - ck[1:2, :]
        if masked:
            ok = _causal_mask(t)
            s0 = jnp.where(ok, s0, NEG)
            s1 = jnp.where(ok, s1, NEG)
        return s0, s1

    def consume(j, s, state):
        v0, v1 = _pair_values(_key_tile(v_ref, j, t))
        return _pair_consume(s, state, v0, v1, (cq0, cq1))

    _, _, acc0, acc1 = _flash_tiles(i, logits, consume, _pair_init(t))
    o_ref[0] = _pair_finish(acc0, acc1).astype(o_ref.dtype)


def _diff_attn_kernel(q_ref, k_ref, v_ref, lam_ref, g_ref, o_ref, *, t, lambda_init):
    i = pl.program_id(2)
    q0, q1 = _split_pair(q_ref[0])

    def logits(j, masked):
        kj = _key_tile(k_ref, j, t)
        s0 = _dot_nt(q0, kj)
        s1 = _dot_nt(q1, kj)
        if masked:
            ok = _causal_mask(t)
            s0 = jnp.where(ok, s0, NEG)
            s1 = jnp.where(ok, s1, NEG)
        return s0, s1

    def consume(j, s, state):
        m0, m1, acc0, acc1 = state
        vj = _key_tile(v_ref, j, t)
        v_ext = jnp.concatenate([vj, jnp.ones_like(vj)], axis=1)
        p0, a0, m0 = _softmax_tile(s[0], m0)
        p1, a1, m1 = _softmax_tile(s[1], m1)
        acc0 = _lane_tile(a0, 2 * LANES) * acc0 + _dot(p0, v_ext)
        acc1 = _lane_tile(a1, 2 * LANES) * acc1 + _dot(p1, v_ext)
        return m0, m1, acc0, acc1

    m_init = jnp.full((t, LANES), NEG, F32)
    zacc = jnp.zeros((t, 2 * LANES), F32)
    _, _, acc0, acc1 = _flash_tiles(i, logits, consume, (m_init, m_init, zacc, zacc))
    lam = lam_ref[...]
    lam_val = (jnp.exp(jnp.sum(lam[0:1] * lam[1:2], axis=-1, keepdims=True))
               - jnp.exp(jnp.sum(lam[2:3] * lam[3:4], axis=-1, keepdims=True)) + lambda_init)
    out = acc0[:, :LANES] / acc0[:, LANES:] - lam_val * (acc1[:, :LANES] / acc1[:, LANES:])
    out = out * lax.rsqrt(jnp.mean(out * out, axis=-1, keepdims=True) + RMS_EPS) * g_ref[...]
    o_ref[0] = (out * (1.0 - lambda_init)).astype(o_ref.dtype)


def _swa_attn_kernel(sink_ref, q_ref, k_ref, v_ref, o_ref, *, t, blk):
    hp = pl.program_id(1)
    i = pl.program_id(2)
    low = _lane_is_low((blk, LANES))
    sink0 = sink_ref[2 * hp] * LOG2E
    sink1 = sink_ref[2 * hp + 1] * LOG2E
    row = lax.broadcasted_iota(jnp.int32, (blk, 2 * blk), 0)
    col = lax.broadcasted_iota(jnp.int32, (blk, 2 * blk), 1)
    for u in range(t // blk):
        r0 = i * t + u * blk
        k0 = pl.multiple_of(jnp.maximum(r0 - blk, 0), blk)
        q0, q1 = _split_pair(q_ref[0, u * blk:(u + 1) * blk, :])
        kj = k_ref[0, pl.ds(k0, 2 * blk), :]
        vj = v_ref[0, pl.ds(k0, 2 * blk), :]
        dist = (r0 + row) - (k0 + col)
        ok = (dist >= 0) & (dist < C_WINDOW)
        outs = []
        for qe, sink in ((q0, sink0), (q1, sink1)):
            s = jnp.where(ok, _dot_nt(qe, kj), NEG)
            m = jnp.maximum(jnp.max(s, axis=-1, keepdims=True), sink)
            p = jnp.exp2(s - m)
            denom = jnp.sum(p, axis=-1, keepdims=True) + jnp.exp2(sink - m)
            outs.append(_dot(p.astype(BF16), vj) / denom)
        o_ref[0, u * blk:(u + 1) * blk, :] = jnp.where(low, outs[0], outs[1]).astype(o_ref.dtype)


def _sortable(score):
    bits = lax.bitcast_convert_type(score, jnp.int32)
    return bits ^ (lax.shift_right_arithmetic(bits, 31) & jnp.int32(0x7FFFFFFF))


def _sublane_fold(x):
    return jnp.sum(x.reshape(x.shape[0] // 8, 8, x.shape[1]), axis=0)


def _dsa_attn_kernel(q_ref, k_ref, v_ref, qi_ref, ki_ref, wi_ref, o_ref, keys_ref, selb_ref, *, t, k_sel):
    i = pl.program_id(1)
    hp = pl.program_id(2)

    def count_keys(pred_fn):
        def body(j, acc):
            return acc + _sublane_fold(jnp.where(pred_fn(keys_ref[j]), 1.0, 0.0))
        acc = lax.fori_loop(0, i + 1, body, jnp.zeros((8, t), F32))
        return jnp.sum(acc, axis=0, keepdims=True)

    @pl.when(hp == 0)
    def _select():
        qi = qi_ref[0]
        lane = lax.broadcasted_iota(jnp.int32, qi.shape, 1)
        qis = [jnp.where((lane >= h * A_IDX_DIM) & (lane < (h + 1) * A_IDX_DIM), qi, jnp.zeros_like(qi))
               for h in range(A_IDX_HEADS)]
        w_t = jnp.transpose(wi_ref[0])
        key_pos = lax.broadcasted_iota(jnp.int32, (t, t), 0)
        qry_pos = lax.broadcasted_iota(jnp.int32, (t, t), 1)

        def score_tile(j, masked):
            kij = ki_ref[0, pl.ds(pl.multiple_of(j * t, t), t), :]
            score = jnp.zeros((t, t), F32)
            for h in range(A_IDX_HEADS):
                score = score + w_t[h:h + 1, :] * jnp.maximum(_dot_nt(kij, qis[h]), 0.0)
            key = _sortable(score)
            if masked:
                key = jnp.where(key_pos <= qry_pos, key, INT_MIN)
            keys_ref[j] = key

        def score_body(j, c):
            score_tile(j, False)
            return c
        lax.fori_loop(0, i, score_body, 0)
        score_tile(i, True)

        def bit_body(b, thr_u):
            cand = thr_u | lax.shift_left(jnp.int32(1), 31 - b)
            cand_s = cand ^ INT_MIN
            cnt = count_keys(lambda key: key >= cand_s)
            return jnp.where(cnt >= k_sel, cand, thr_u)
        thr_u = lax.fori_loop(0, 32, bit_body, jnp.zeros((1, t), jnp.int32))
        thr = thr_u ^ INT_MIN
        need = k_sel - count_keys(lambda key: key > thr)

        earlier = (qry_pos < key_pos).astype(BF16)

        def sel_tile(j, seen, masked):
            key = keys_ref[j]
            eq = key == thr
            eq_f = jnp.where(eq, 1.0, 0.0)
            rank = _dot(earlier, eq_f.astype(BF16)) + seen
            sel = (key > thr) | (eq & (rank < need))
            if masked:
                sel = sel & (key_pos <= qry_pos)
            selb_ref[j] = jnp.transpose(jnp.where(sel, 0.0, NEG))
            return seen + jnp.sum(_sublane_fold(eq_f), axis=0, keepdims=True)

        seen = lax.fori_loop(0, i, lambda j, c: sel_tile(j, c, False), jnp.zeros((1, t), F32))
        sel_tile(i, seen, True)

    q0, q1 = _split_pair(q_ref[0])

    def logits(j, _):
        kj = _key_tile(k_ref, j, t)
        bias = selb_ref[j]
        return _dot_nt(q0, kj) + bias, _dot_nt(q1, kj) + bias

    def consume(j, s, state):
        v0, v1 = _pair_values(_key_tile(v_ref, j, t))
        return _pair_consume(s, state, v0, v1)

    _, _, acc0, acc1 = _flash_tiles(i, logits, consume, _pair_init(t))
    o_ref[0] = _pair_finish(acc0, acc1).astype(o_ref.dtype)


def _rope_lane_tables(seq, rot_dim, period):
    half = rot_dim // 2
    inv = ROPE_THETA ** (-jnp.arange(0, rot_dim, 2, dtype=F32) / rot_dim)
    ang = jnp.arange(seq, dtype=F32)[:, None] * inv[None, :]
    cos, sin = jnp.cos(ang), jnp.sin(ang)
    lane = jnp.arange(LANES) % period
    idx = jnp.where(lane < half, lane, jnp.clip(lane - half, 0, half - 1))
    first = (lane < half)[None, :]
    second = ((lane >= half) & (lane < 2 * half))[None, :]
    c = jnp.where(first | second, cos[:, idx], 1.0)
    s_up = jnp.where(first, -sin[:, idx], 0.0)
    s_dn = jnp.where(second, sin[:, idx], 0.0)
    return c.astype(F32), s_up.astype(F32), s_dn.astype(F32)


def _pad_cols(w, n):
    return jnp.pad(w, ((0, 0), (0, n - w.shape[1])))


def _attn_tile(s):
    return min(512, s)


def _mixer_dsa(x, mod, w_in, kv_norm, w_kv_up, tabs, tabs_i, *, tm):
    bsz, s, d = x.shape
    t = _attn_tile(s)
    k_sel = min(A_TOPK_MAX, s // 4)
    n_qi = A_IDX_HEADS * A_IDX_DIM
    c0 = d + A_KV_RANK
    w_ki = w_in[:, c0 + n_qi:c0 + n_qi + A_IDX_DIM]
    w_wi = w_in[:, c0 + n_qi + A_IDX_DIM:]
    w_cat = jnp.concatenate([w_in[:, :c0 + n_qi], jnp.tile(w_ki, (1, n_qi // A_IDX_DIM)),
                             _pad_cols(w_wi, LANES)], axis=1).astype(BF16)
    wk, wv = w_kv_up[:, :HEAD_DIM], w_kv_up[:, HEAD_DIM:]
    wkv = jnp.concatenate([wk, wk, wv, wv], axis=1).astype(BF16)
    q, k2, v2, qi, ki, wi = _proj_call(
        functools.partial(_dsa_proj_kernel, d=d), x, mod,
        [w_cat, kv_norm.reshape(1, -1), wkv], list(tabs) + list(tabs_i),
        [(d, BF16), (LANES, BF16), (LANES, BF16), (n_qi, BF16), (n_qi, BF16), (LANES, F32)],
        tm=tm, name="dsa_in_proj")
    nt = s // t
    return pl.pallas_call(
        functools.partial(_dsa_attn_kernel, t=t, k_sel=k_sel),
        out_shape=jax.ShapeDtypeStruct((bsz, s, d), BF16),
        grid=(bsz, nt, d // LANES),
        in_specs=[pl.BlockSpec((1, t, LANES), lambda b, i, h: (b, i, h)),
                  pl.BlockSpec((1, s, LANES), lambda b, i, h: (b, 0, 0)),
                  pl.BlockSpec((1, s, LANES), lambda b, i, h: (b, 0, 0)),
                  pl.BlockSpec((1, t, n_qi), lambda b, i, h: (b, i, 0)),
                  pl.BlockSpec((1, s, n_qi), lambda b, i, h: (b, 0, 0)),
                  pl.BlockSpec((1, t, LANES), lambda b, i, h: (b, i, 0))],
        out_specs=pl.BlockSpec((1, t, LANES), lambda b, i, h: (b, i, h)),
        scratch_shapes=[pltpu.VMEM((nt, t, t), jnp.int32), pltpu.VMEM((nt, t, t), F32)],
        compiler_params=_cparams(3),
        name="dsa_attention",
    )(q, k2, v2, qi, ki, wi)


def _mixer_fox(x, mod, w_in, f_bias, *, tm):
    bsz, s, d = x.shape
    t = _attn_tile(s)
    nh = d // HEAD_DIM
    w_cat = _pad_cols(w_in, 3 * d + LANES).astype(BF16)
    fb = _pad_cols(f_bias.reshape(1, -1), LANES)
    q, k, v, cum = _proj_call(
        functools.partial(_fox_proj_kernel, d=d, tm=tm), x, mod, [w_cat, fb], [],
        [(d, BF16), (d, BF16), (d, BF16), (LANES, F32)],
        tm=tm, scratch=[pltpu.VMEM((1, LANES), F32)], name="fox_in_proj")
    nt = s // t
    cum = cum[:, :, :nh].reshape(bsz, s, nh // 2, 2)
    cq = jnp.transpose(cum, (0, 2, 1, 3))
    ck = jnp.transpose(cum.reshape(bsz, nt, t, nh // 2, 2), (0, 3, 1, 4, 2))
    return pl.pallas_call(
        functools.partial(_fox_attn_kernel, t=t),
        out_shape=jax.ShapeDtypeStruct((bsz, s, d), BF16),
        grid=(bsz, d // LANES, nt),
        in_specs=[pl.BlockSpec((1, t, LANES), lambda b, h, i: (b, i, h)),
                  pl.BlockSpec((1, s, LANES), lambda b, h, i: (b, 0, h)),
                  pl.BlockSpec((1, s, LANES), lambda b, h, i: (b, 0, h)),
                  pl.BlockSpec((1, 1, t, 2), lambda b, h, i: (b, h, i, 0)),
                  pl.BlockSpec((1, 1, nt, 2, t), lambda b, h, i: (b, h, 0, 0, 0))],
        out_specs=pl.BlockSpec((1, t, LANES), lambda b, h, i: (b, i, h)),
        compiler_params=_cparams(3),
        name="fox_attention",
    )(q, k, v, cq, ck)


def _mixer_swa(x, mod, w_in, sinks, tabs, *, tm):
    bsz, s, d = x.shape
    t = _attn_tile(s)
    n_kv = (w_in.shape[1] - d) // (2 * HEAD_DIM)
    pairs_per_kv = (d // LANES) // n_kv
    wq, wk, wv = w_in[:, :d], w_in[:, d:d + n_kv * HEAD_DIM], w_in[:, d + n_kv * HEAD_DIM:]

    def dup(w):
        w = w.reshape(w.shape[0], n_kv, 1, HEAD_DIM)
        return jnp.broadcast_to(w, (w.shape[0], n_kv, 2, HEAD_DIM)).reshape(w.shape[0], n_kv * LANES)

    w_cat = jnp.concatenate([wq, dup(wk), dup(wv)], axis=1).astype(BF16)
    q, k2, v2 = _proj_call(
        functools.partial(_swa_proj_kernel, d=d), x, mod, [w_cat], list(tabs),
        [(d, BF16), (n_kv * LANES, BF16), (n_kv * LANES, BF16)], tm=tm, name="swa_in_proj")
    return pl.pallas_call(
        functools.partial(_swa_attn_kernel, t=t, blk=C_WINDOW),
        out_shape=jax.ShapeDtypeStruct((bsz, s, d), BF16),
        grid=(bsz, d // LANES, s // t),
        in_specs=[pl.BlockSpec(memory_space=pltpu.SMEM),
                  pl.BlockSpec((1, t, LANES), lambda b, h, i: (b, i, h)),
                  pl.BlockSpec((1, s, LANES), lambda b, h, i: (b, 0, h // pairs_per_kv)),
                  pl.BlockSpec((1, s, LANES), lambda b, h, i: (b, 0, h // pairs_per_kv))],
        out_specs=pl.BlockSpec((1, t, LANES), lambda b, h, i: (b, i, h)),
        compiler_params=_cparams(3),
        name="swa_attention",
    )(sinks.astype(F32), q, k2, v2)


def _mixer_diff(x, mod, w_in, lam, subln_g, tabs, lambda_init, *, tm):
    bsz, s, d = x.shape
    t = _attn_tile(s)
    q, k, v = _proj_call(
        functools.partial(_diff_proj_kernel, d=d), x, mod, [w_in.astype(BF16)], list(tabs),
        [(d, BF16), (d, BF16), (d, BF16)], tm=tm, name="diff_in_proj")
    return pl.pallas_call(
        functools.partial(_diff_attn_kernel, t=t, lambda_init=lambda_init),
        out_shape=jax.ShapeDtypeStruct((bsz, s, d), BF16),
        grid=(bsz, d // LANES, s // t),
        in_specs=[pl.BlockSpec((1, t, LANES), lambda b, h, i: (b, i, h)),
                  pl.BlockSpec((1, s, LANES), lambda b, h, i: (b, 0, h)),
                  pl.BlockSpec((1, s, LANES), lambda b, h, i: (b, 0, h)),
                  pl.BlockSpec(lam.shape, lambda b, h, i: (0, 0)),
                  pl.BlockSpec((1, LANES), lambda b, h, i: (0, 0))],
        out_specs=pl.BlockSpec((1, t, LANES), lambda b, h, i: (b, i, h)),
        compiler_params=_cparams(3),
        name="diff_attention",
    )(q, k, v, lam.astype(F32), subln_g.reshape(1, -1))


def kernel(x, c, ln_g, ln_b, w_ada, b_ada, w_ffn_in, w_ffn_out, dsa_w_in, dsa_kv_norm, dsa_w_kv_up, dsa_w_out, fox_w_in, fox_f_bias, fox_w_out, swa_w_in, swa_sinks, swa_w_out, diff_w_in, diff_lambda, diff_subln, diff_w_out):
    bsz, s, d = x.shape
    depth = w_ada.shape[0]
    n_mixers = 4
    alpha = (2 * depth) ** 0.25
    tm = min(512, s)
    tabs = _rope_lane_tables(s, ROT_DIM, HEAD_DIM)
    tabs_i = _rope_lane_tables(s, A_IDX_DIM // 4, A_IDX_DIM)
    mods = _ada_call(c, w_ada, b_ada).reshape(depth, bsz, N_ADA, d)
    for i in range(depth):
        m, r = i % n_mixers, i // n_mixers
        mod = mods[i]
        x = _ffn_call(x, mod, w_ffn_in[i, 0].astype(BF16), w_ffn_out[i, 0].astype(BF16),
                      ln_g[i, 0], ln_b[i, 0], j=0, alpha=alpha, tm=tm)
        if m == 0:
            a = _mixer_dsa(x, mod, dsa_w_in[r], dsa_kv_norm[r], dsa_w_kv_up[r], tabs, tabs_i, tm=tm)
            w_out = dsa_w_out[r]
        elif m == 1:
            a = _mixer_fox(x, mod, fox_w_in[r], fox_f_bias[r], tm=tm)
            w_out = fox_w_out[r]
        elif m == 2:
            a = _mixer_swa(x, mod, swa_w_in[r], swa_sinks[r], tabs, tm=tm)
            w_out = swa_w_out[r]
        else:
            lambda_init = 0.8 - 0.6 * math.exp(-0.3 * i)
            a = _mixer_diff(x, mod, diff_w_in[r], diff_lambda[r], diff_subln[r], tabs, lambda_init, tm=tm)
            w_out = diff_w_out[r]
        x = _oproj_call(a, x, mod, w_out.astype(BF16), ln_g[i, 1], ln_b[i, 1], alpha=alpha, tm=tm)
        x = _ffn_call(x, mod, w_ffn_in[i, 1].astype(BF16), w_ffn_out[i, 1].astype(BF16),
                      ln_g[i, 2], ln_b[i, 2], j=2, alpha=alpha, tm=tm)
    return x
```

```python
import functools
import math

import jax
import jax.numpy as jnp
from jax import lax
from jax.experimental import pallas as pl
from jax.experimental.pallas import tpu as pltpu

F32 = jnp.float32
BF16 = jnp.bfloat16

HEAD_DIM = 64
ROT_DIM = HEAD_DIM // 4
ROPE_THETA = 500000.0
LN_EPS = 1e-5
RMS_EPS = 1e-6
N_ADA = 9
A_KV_RANK = 128
A_IDX_HEADS = 8
A_IDX_DIM = 32
A_TOPK_MAX = 256
C_WINDOW = 128
LANES = 128
NEG = -1e30
INT_MIN = -(2 ** 31)
LOG2E = math.log2(math.e)
Q_SCALE = HEAD_DIM ** -0.5 * LOG2E
VMEM_LIMIT = 56 * 1024 * 1024


def _cparams(n_axes):
    return pltpu.CompilerParams(dimension_semantics=("arbitrary",) * n_axes,
                                vmem_limit_bytes=VMEM_LIMIT)


def _resident(shape):
    nd = len(shape)
    return pl.BlockSpec(shape, lambda *_: (0,) * nd, pipeline_mode=pl.Buffered(1))


def _dot(a, b):
    return jnp.dot(a, b, preferred_element_type=F32)


def _dot_nt(a, b):
    return lax.dot_general(a, b, (((1,), (1,)), ((), ())), preferred_element_type=F32)


def _sigmoid(x):
    return 1.0 / (1.0 + jnp.exp(-x))


def _layer_norm(z, g, b):
    mu = jnp.mean(z, axis=-1, keepdims=True)
    zc = z - mu
    var = jnp.mean(zc * zc, axis=-1, keepdims=True)
    return zc * lax.rsqrt(var + LN_EPS) * g + b


def _modulate(x, mod_ref, j):
    shift = mod_ref[0, 3 * j:3 * j + 1, :]
    scale = mod_ref[0, 3 * j + 1:3 * j + 2, :]
    return (x * (1.0 + scale) + shift).astype(BF16)


def _rope(xs, c, s_up, s_dn, half):
    return (xs * c + pltpu.roll(xs, LANES - half, 1) * s_up + pltpu.roll(xs, half, 1) * s_dn)


def _split_bf16(x):
    hi = x.astype(BF16)
    return hi, (x - hi.astype(F32)).astype(BF16)


def _ada_kernel(c_ref, w_ref, b_ref, o_ref):
    c = c_ref[...]
    c_hi, c_lo = _split_bf16(c * _sigmoid(c))
    w_hi, w_lo = _split_bf16(w_ref[0])
    o_ref[0] = _dot(c_hi, w_hi) + (_dot(c_hi, w_lo) + _dot(c_lo, w_hi)) + b_ref[0]


def _ada_call(c, w_ada, b_ada):
    depth, d, nd = w_ada.shape
    b = c.shape[0]
    n_blk = nd // d
    return pl.pallas_call(
        _ada_kernel,
        out_shape=jax.ShapeDtypeStruct((depth, b, nd), F32),
        grid=(depth, n_blk),
        in_specs=[pl.BlockSpec((b, d), lambda l, n: (0, 0)),
                  pl.BlockSpec((1, d, d), lambda l, n: (l, 0, n)),
                  pl.BlockSpec((1, 1, d), lambda l, n: (l, 0, n))],
        out_specs=pl.BlockSpec((1, b, d), lambda l, n: (l, 0, n)),
        compiler_params=_cparams(2),
        name="ada_mod",
    )(c, w_ada, b_ada.reshape(depth, 1, nd))


FFN_ROWS = 1024
FFN_SUB = 512
FFN_CHUNK = 256


def _swiglu_rows(h, win_ref, wout_ref, dff):
    y = jnp.zeros((h.shape[0], wout_ref.shape[1]), F32)
    a_prev = None
    for c0 in range(0, dff, FFN_CHUNK):
        g = _dot(h, win_ref[:, c0:c0 + FFN_CHUNK])
        u = _dot(h, win_ref[:, dff + c0:dff + c0 + FFN_CHUNK])
        if a_prev is not None:
            y = y + _dot(a_prev, wout_ref[c0 - FFN_CHUNK:c0, :])
        a_prev = (g * _sigmoid(g) * u).astype(BF16)
    return y + _dot(a_prev, wout_ref[dff - FFN_CHUNK:dff, :])


def _ffn_kernel(x_ref, mod_ref, win_ref, wout_ref, g_ref, b_ref, o_ref, *, j, dff, alpha):
    gate = mod_ref[0, 3 * j + 2:3 * j + 3, :]
    for r0 in range(0, x_ref.shape[1], FFN_SUB):
        x = x_ref[0, r0:r0 + FFN_SUB, :]
        y = _swiglu_rows(_modulate(x, mod_ref, j), win_ref, wout_ref, dff)
        z = alpha * x + 0.5 * (1.0 + gate) * y
        o_ref[0, r0:r0 + FFN_SUB, :] = _layer_norm(z, g_ref[...], b_ref[...])


def _mixer_out_ffn_kernel(a_ref, x_ref, mod_ref, wo_ref, g1_ref, b1_ref, win_ref, wout_ref, g2_ref, b2_ref,
                          o_ref, *, dff, alpha):
    subs = range(0, x_ref.shape[1], FFN_SUB)
    ys = [_dot(a_ref[0, r0:r0 + FFN_SUB, :], wo_ref[...]) for r0 in subs]
    gate1 = mod_ref[0, 5:6, :]
    gate2 = mod_ref[0, 8:9, :]
    for r0, y1 in zip(subs, ys):
        x = _layer_norm(alpha * x_ref[0, r0:r0 + FFN_SUB, :] + (1.0 + gate1) * y1, g1_ref[...], b1_ref[...])
        y = _swiglu_rows(_modulate(x, mod_ref, 2), win_ref, wout_ref, dff)
        z = alpha * x + 0.5 * (1.0 + gate2) * y
        o_ref[0, r0:r0 + FFN_SUB, :] = _layer_norm(z, g2_ref[...], b2_ref[...])


def _ffn_rows(s):
    return min(FFN_ROWS, s)


def _ffn_call(x, mod, w_in, w_out, g, b, *, j, alpha):
    bsz, s, d = x.shape
    dff = w_out.shape[0]
    tm = _ffn_rows(s)
    return pl.pallas_call(
        functools.partial(_ffn_kernel, j=j, dff=dff, alpha=alpha),
        out_shape=jax.ShapeDtypeStruct(x.shape, F32),
        grid=(bsz, s // tm),
        in_specs=[pl.BlockSpec((1, tm, d), lambda bi, si: (bi, si, 0)),
                  pl.BlockSpec((1, N_ADA, d), lambda bi, si: (bi, 0, 0)),
                  _resident(w_in.shape), _resident(w_out.shape),
                  _resident((1, d)), _resident((1, d))],
        out_specs=pl.BlockSpec((1, tm, d), lambda bi, si: (bi, si, 0)),
        compiler_params=_cparams(2),
        name="ffn_sublayer",
    )(x, mod, w_in, w_out, g.reshape(1, d), b.reshape(1, d))


def _mixer_out_ffn_call(a, x, mod, w_o, g1, b1, w_in, w_out, g2, b2, *, alpha):
    bsz, s, d = x.shape
    dff = w_out.shape[0]
    tm = _ffn_rows(s)
    row_block = lambda width: pl.BlockSpec((1, tm, width), lambda bi, si: (bi, si, 0))
    return pl.pallas_call(
        functools.partial(_mixer_out_ffn_kernel, dff=dff, alpha=alpha),
        out_shape=jax.ShapeDtypeStruct(x.shape, F32),
        grid=(bsz, s // tm),
        in_specs=[row_block(a.shape[-1]), row_block(d),
                  pl.BlockSpec((1, N_ADA, d), lambda bi, si: (bi, 0, 0)),
                  _resident(w_o.shape), _resident((1, d)), _resident((1, d)),
                  _resident(w_in.shape), _resident(w_out.shape),
                  _resident((1, d)), _resident((1, d))],
        out_specs=row_block(d),
        compiler_params=_cparams(2),
        name="mixer_out_ffn",
    )(a, x, mod, w_o, g1.reshape(1, d), b1.reshape(1, d), w_in, w_out, g2.reshape(1, d), b2.reshape(1, d))


def _rope_store(dst_ref, p, col0, n_blk, tabs, half, scale=None):
    c, s_up, s_dn = tabs
    for i in range(n_blk):
        xs = _rope(p[:, col0 + i * LANES:col0 + (i + 1) * LANES], c, s_up, s_dn, half)
        if scale is not None:
            xs = xs * scale
        dst_ref[0, :, i * LANES:(i + 1) * LANES] = xs.astype(dst_ref.dtype)


def _plain_store(dst_ref, p, col0, n_blk):
    dst_ref[0] = p[:, col0:col0 + n_blk * LANES].astype(dst_ref.dtype)


def _dsa_proj_kernel(x_ref, mod_ref, w_ref, kvn_ref, wkv_ref, c_ref, su_ref, sd_ref,
                     ci_ref, sui_ref, sdi_ref,
                     q_ref, k_ref, v_ref, qi_ref, ki_ref, wi_ref, *, d):
    h = _modulate(x_ref[0], mod_ref, 1)
    p = _dot(h, w_ref[...])
    tabs = (c_ref[...], su_ref[...], sd_ref[...])
    tabs_i = (ci_ref[...], sui_ref[...], sdi_ref[...])
    nq = d // LANES
    _rope_store(q_ref, p, 0, nq, tabs, ROT_DIM // 2, scale=Q_SCALE)
    ckv = p[:, d:d + A_KV_RANK]
    ckv = ckv * lax.rsqrt(jnp.mean(ckv * ckv, axis=-1, keepdims=True) + RMS_EPS) * kvn_ref[...]
    kv = _dot(ckv.astype(BF16), wkv_ref[...])
    _rope_store(k_ref, kv, 0, 1, tabs, ROT_DIM // 2)
    _plain_store(v_ref, kv, LANES, 1)
    c0 = d + A_KV_RANK
    _rope_store(qi_ref, p, c0, 2, tabs_i, A_IDX_DIM // 8)
    _rope_store(ki_ref, p, c0 + 2 * LANES, 2, tabs_i, A_IDX_DIM // 8)
    wi_ref[0] = p[:, c0 + 4 * LANES:c0 + 5 * LANES] * (A_IDX_HEADS ** -0.5 * A_IDX_DIM ** -0.5)


def _fox_proj_kernel(x_ref, mod_ref, w_ref, fb_ref, q_ref, k_ref, v_ref, cum_ref, carry_ref, *, d, tm):
    h = _modulate(x_ref[0], mod_ref, 1)
    p = _dot(h, w_ref[...])
    q_ref[0] = (p[:, 0:d] * Q_SCALE).astype(BF16)
    k_ref[0] = p[:, d:2 * d].astype(BF16)
    v_ref[0] = p[:, 2 * d:3 * d].astype(BF16)
    f = p[:, 3 * d:3 * d + LANES] + fb_ref[...]
    logf = jnp.minimum(f, 0.0) - jnp.log1p(jnp.exp(-jnp.abs(f)))

    @pl.when(pl.program_id(1) == 0)
    def _():
        carry_ref[...] = jnp.zeros_like(carry_ref)

    hi = logf.astype(BF16)
    r1 = logf - hi.astype(F32)
    mid = r1.astype(BF16)
    lo = (r1 - mid.astype(F32)).astype(BF16)
    row = lax.broadcasted_iota(jnp.int32, (tm, tm), 0)
    col = lax.broadcasted_iota(jnp.int32, (tm, tm), 1)
    tri = (row >= col).astype(BF16)
    cum = _dot(tri, hi) + _dot(tri, mid) + _dot(tri, lo) + carry_ref[...]
    cum_ref[0] = cum * LOG2E
    carry_ref[...] = cum[tm - 1:tm, :]


def _swa_proj_kernel(x_ref, mod_ref, w_ref, c_ref, su_ref, sd_ref, q_ref, k_ref, v_ref, *, d):
    h = _modulate(x_ref[0], mod_ref, 1)
    p = _dot(h, w_ref[...])
    tabs = (c_ref[...], su_ref[...], sd_ref[...])
    nq = d // LANES
    _rope_store(q_ref, p, 0, nq, tabs, ROT_DIM // 2, scale=Q_SCALE)
    _rope_store(k_ref, p, d, 2, tabs, ROT_DIM // 2)
    _plain_store(v_ref, p, d + 2 * LANES, 2)


def _diff_proj_kernel(x_ref, mod_ref, w_ref, c_ref, su_ref, sd_ref, q_ref, k_ref, v_ref, *, d):
    h = _modulate(x_ref[0], mod_ref, 1)
    p = _dot(h, w_ref[...])
    tabs = (c_ref[...], su_ref[...], sd_ref[...])
    nq = d // LANES
    _rope_store(q_ref, p, 0, nq, tabs, ROT_DIM // 2, scale=Q_SCALE)
    _rope_store(k_ref, p, d, nq, tabs, ROT_DIM // 2)
    _plain_store(v_ref, p, 2 * d, nq)


def _proj_call(kern, x, mod, consts, tabs, outs, *, tm, scratch=(), name):
    bsz, s, d = x.shape
    in_specs = [pl.BlockSpec((1, tm, d), lambda bi, si: (bi, si, 0)),
                pl.BlockSpec((1, N_ADA, d), lambda bi, si: (bi, 0, 0))]
    in_specs += [_resident(a.shape) for a in consts]
    in_specs += [pl.BlockSpec((tm, LANES), lambda bi, si: (si, 0)) for _ in tabs]
    return pl.pallas_call(
        kern,
        out_shape=[jax.ShapeDtypeStruct((bsz, s, w), dt) for w, dt in outs],
        grid=(bsz, s // tm),
        in_specs=in_specs,
        out_specs=[pl.BlockSpec((1, tm, w), lambda bi, si: (bi, si, 0)) for w, _ in outs],
        scratch_shapes=list(scratch),
        compiler_params=_cparams(2),
        name=name,
    )(x, mod, *consts, *tabs)


def _lane_is_low(shape):
    return lax.broadcasted_iota(jnp.int32, shape, 1) < HEAD_DIM


def _split_pair(q):
    low = _lane_is_low(q.shape)
    zero = jnp.zeros_like(q)
    return jnp.where(low, q, zero), jnp.where(low, zero, q)


def _lane_tile(x, width):
    return jnp.concatenate([x] * (width // LANES), axis=1)


def _softmax_tile(s, m, row_term=None):
    tile_max = jnp.broadcast_to(jnp.max(s, axis=-1, keepdims=True), m.shape)
    if row_term is not None:
        tile_max = tile_max + row_term
    m_new = jnp.maximum(m, tile_max)
    shift = m_new if row_term is None else m_new - row_term
    p = jnp.exp2(s - _lane_tile(shift, s.shape[1])).astype(BF16)
    return p, jnp.exp2(m - m_new), m_new


def _causal_mask(n):
    row = lax.broadcasted_iota(jnp.int32, (n, n), 0)
    col = lax.broadcasted_iota(jnp.int32, (n, n), 1)
    return row >= col


FLASH_BLOCK = 4


def _flash_tiles(n_plain, logits, consume, state, last_masked=True):
    def run_block(first, count, mask_last, st):
        s_prev = logits(first, mask_last and count == 1)
        for k in range(1, count):
            s_next = logits(first + k, mask_last and k == count - 1)
            st = consume(first + k - 1, s_prev, st)
            s_prev = s_next
        return consume(first + count - 1, s_prev, st)

    n_full = n_plain // FLASH_BLOCK
    state = lax.fori_loop(0, n_full, lambda j, st: run_block(FLASH_BLOCK * j, FLASH_BLOCK, False, st), state)
    first = n_full * FLASH_BLOCK
    branches = [functools.partial(run_block, first, c, last_masked) for c in range(1, FLASH_BLOCK + 1)]
    return lax.switch(n_plain - first, branches, state)


def _pair_values(vj):
    low = _lane_is_low(vj.shape)
    one = jnp.ones_like(vj)
    return jnp.where(low, vj, one), jnp.where(low, one, vj)


def _pair_init(t):
    m = jnp.full((t, LANES), NEG, F32)
    zacc = jnp.zeros((t, LANES), F32)
    return m, m, zacc, zacc


def _pair_consume(s, state, v0, v1, row_terms=(None, None)):
    m0, m1, acc0, acc1 = state
    p0, a0, m0 = _softmax_tile(s[0], m0, row_terms[0])
    p1, a1, m1 = _softmax_tile(s[1], m1, row_terms[1])
    return m0, m1, a0 * acc0 + _dot(p0, v0), a1 * acc1 + _dot(p1, v1)


def _pair_finish(acc0, acc1):
    low = _lane_is_low(acc0.shape)
    return jnp.where(low, acc0 / pltpu.roll(acc0, HEAD_DIM, 1), acc1 / pltpu.roll(acc1, HEAD_DIM, 1))


def _key_tile(ref, j, t):
    return ref[0, pl.ds(pl.multiple_of(j * t, t), t), :]


def _fox_attn_kernel(q_ref, k_ref, v_ref, cq_ref, ck_ref, o_ref, *, t):
    i = pl.program_id(2)
    q0, q1 = _split_pair(q_ref[0])
    cq = cq_ref[0, 0]
    cq0 = jnp.broadcast_to(cq[:, 0:1], (t, LANES))
    cq1 = jnp.broadcast_to(cq[:, 1:2], (t, LANES))

    def logits(j, masked):
        kj = _key_tile(k_ref, j, t)
        ck = ck_ref[0, 0, j]
        s0 = _dot_nt(q0, kj) - ck[0:1, :]
        s1 = _dot_nt(q1, kj) - ck[1:2, :]
        if masked:
            ok = _causal_mask(t)
            s0 = jnp.where(ok, s0, NEG)
            s1 = jnp.where(ok, s1, NEG)
        return s0, s1

    def consume(j, s, state):
        v0, v1 = _pair_values(_key_tile(v_ref, j, t))
        return _pair_consume(s, state, v0, v1, (cq0, cq1))

    _, _, acc0, acc1 = _flash_tiles(i, logits, consume, _pair_init(t))
    o_ref[0] = _pair_finish(acc0, acc1).astype(o_ref.dtype)


def _diff_attn_kernel(q_ref, k_ref, v_ref, lam_ref, g_ref, o_ref, *, t, lambda_init):
    i = pl.program_id(2)
    q0, q1 = _split_pair(q_ref[0])

    def logits(j, masked):
        kj = _key_tile(k_ref, j, t)
        s0 = _dot_nt(q0, kj)
        s1 = _dot_nt(q1, kj)
        if masked:
            ok = _causal_mask(t)
            s0 = jnp.where(ok, s0, NEG)
            s1 = jnp.where(ok, s1, NEG)
        return s0, s1

    def consume(j, s, state):
        m0, m1, acc0, acc1 = state
        vj = _key_tile(v_ref, j, t)
        v_ext = jnp.concatenate([vj, jnp.ones_like(vj)], axis=1)
        p0, a0, m0 = _softmax_tile(s[0], m0)
        p1, a1, m1 = _softmax_tile(s[1], m1)
        acc0 = _lane_tile(a0, 2 * LANES) * acc0 + _dot(p0, v_ext)
        acc1 = _lane_tile(a1, 2 * LANES) * acc1 + _dot(p1, v_ext)
        return m0, m1, acc0, acc1

    m_init = jnp.full((t, LANES), NEG, F32)
    zacc = jnp.zeros((t, 2 * LANES), F32)
    _, _, acc0, acc1 = _flash_tiles(i, logits, consume, (m_init, m_init, zacc, zacc))
    lam = lam_ref[...]
    lam_val = (jnp.exp(jnp.sum(lam[0:1] * lam[1:2], axis=-1, keepdims=True))
               - jnp.exp(jnp.sum(lam[2:3] * lam[3:4], axis=-1, keepdims=True)) + lambda_init)
    out = acc0[:, :LANES] / acc0[:, LANES:] - lam_val * (acc1[:, :LANES] / acc1[:, LANES:])
    out = out * lax.rsqrt(jnp.mean(out * out, axis=-1, keepdims=True) + RMS_EPS) * g_ref[...]
    o_ref[0] = (out * (1.0 - lambda_init)).astype(o_ref.dtype)


def _swa_attn_kernel(sink_ref, q_ref, k_ref, v_ref, o_ref, *, t, blk):
    hp = pl.program_id(1)
    i = pl.program_id(2)
    low = _lane_is_low((blk, LANES))
    sink0 = sink_ref[2 * hp] * LOG2E
    sink1 = sink_ref[2 * hp + 1] * LOG2E
    row = lax.broadcasted_iota(jnp.int32, (blk, 2 * blk), 0)
    col = lax.broadcasted_iota(jnp.int32, (blk, 2 * blk), 1)
    for u in range(t // blk):
        r0 = i * t + u * blk
        k0 = pl.multiple_of(jnp.maximum(r0 - blk, 0), blk)
        q0, q1 = _split_pair(q_ref[0, u * blk:(u + 1) * blk, :])
        kj = k_ref[0, pl.ds(k0, 2 * blk), :]
        vj = v_ref[0, pl.ds(k0, 2 * blk), :]
        dist = (r0 + row) - (k0 + col)
        ok = (dist >= 0) & (dist < C_WINDOW)
        outs = []
        for qe, sink in ((q0, sink0), (q1, sink1)):
            s = jnp.where(ok, _dot_nt(qe, kj), NEG)
            m = jnp.maximum(jnp.max(s, axis=-1, keepdims=True), sink)
            p = jnp.exp2(s - m)
            denom = jnp.sum(p, axis=-1, keepdims=True) + jnp.exp2(sink - m)
            outs.append(_dot(p.astype(BF16), vj) / denom)
        o_ref[0, u * blk:(u + 1) * blk, :] = jnp.where(low, outs[0], outs[1]).astype(o_ref.dtype)


def _sortable(score):
    bits = lax.bitcast_convert_type(score, jnp.int32)
    return bits ^ (lax.shift_right_arithmetic(bits, 31) & jnp.int32(0x7FFFFFFF))


def _sublane_fold(x):
    return jnp.sum(x.reshape(x.shape[0] // 8, 8, x.shape[1]), axis=0)


def _dsa_attn_kernel(q_ref, k_ref, v_ref, qi_ref, ki_ref, wi_ref, o_ref, keys_ref, selb_ref, *, t, k_sel):
    i = pl.program_id(1)
    hp = pl.program_id(2)

    def count_keys(pred_fn):
        def body(j, acc):
            return acc + _sublane_fold(jnp.where(pred_fn(keys_ref[j]), 1.0, 0.0))
        acc = lax.fori_loop(0, i + 1, body, jnp.zeros((8, t), F32))
        return jnp.sum(acc, axis=0, keepdims=True)

    @pl.when(hp == 0)
    def _select():
        qi = qi_ref[0]
        lane = lax.broadcasted_iota(jnp.int32, qi.shape, 1)
        qis = [jnp.where((lane >= h * A_IDX_DIM) & (lane < (h + 1) * A_IDX_DIM), qi, jnp.zeros_like(qi))
               for h in range(A_IDX_HEADS)]
        w_t = jnp.transpose(wi_ref[0])
        key_pos = lax.broadcasted_iota(jnp.int32, (t, t), 0)
        qry_pos = lax.broadcasted_iota(jnp.int32, (t, t), 1)

        def score_tile(j, masked):
            kij = ki_ref[0, pl.ds(pl.multiple_of(j * t, t), t), :]
            score = jnp.zeros((t, t), F32)
            for h in range(A_IDX_HEADS):
                score = score + w_t[h:h + 1, :] * jnp.maximum(_dot_nt(kij, qis[h]), 0.0)
            key = _sortable(score)
            if masked:
                key = jnp.where(key_pos <= qry_pos, key, INT_MIN)
            keys_ref[j] = key

        def score_body(j, c):
            score_tile(j, False)
            return c
        lax.fori_loop(0, i, score_body, 0)
        score_tile(i, True)

        def bit_body(b, thr_u):
            cand = thr_u | lax.shift_left(jnp.int32(1), 31 - b)
            cand_s = cand ^ INT_MIN
            cnt = count_keys(lambda key: key >= cand_s)
            return jnp.where(cnt >= k_sel, cand, thr_u)
        thr_u = lax.fori_loop(0, 32, bit_body, jnp.zeros((1, t), jnp.int32))
        thr = thr_u ^ INT_MIN
        need = k_sel - count_keys(lambda key: key > thr)

        earlier = (qry_pos < key_pos).astype(BF16)

        def sel_tile(j, seen, masked):
            key = keys_ref[j]
            eq = key == thr
            eq_f = jnp.where(eq, 1.0, 0.0)
            rank = _dot(earlier, eq_f.astype(BF16)) + seen
            sel = (key > thr) | (eq & (rank < need))
            if masked:
                sel = sel & (key_pos <= qry_pos)
            selb_ref[j] = jnp.transpose(jnp.where(sel, 0.0, NEG))
            return seen + jnp.sum(_sublane_fold(eq_f), axis=0, keepdims=True)

        seen = lax.fori_loop(0, i, lambda j, c: sel_tile(j, c, False), jnp.zeros((1, t), F32))
        sel_tile(i, seen, True)

    q0, q1 = _split_pair(q_ref[0])

    def logits(j, _):
        kj = _key_tile(k_ref, j, t)
        bias = selb_ref[j]
        return _dot_nt(q0, kj) + bias, _dot_nt(q1, kj) + bias

    def consume(j, s, state):
        v0, v1 = _pair_values(_key_tile(v_ref, j, t))
        return _pair_consume(s, state, v0, v1)

    _, _, acc0, acc1 = _flash_tiles(i, logits, consume, _pair_init(t))
    o_ref[0] = _pair_finish(acc0, acc1).astype(o_ref.dtype)


def _rope_lane_tables(seq, rot_dim, period):
    half = rot_dim // 2
    inv = ROPE_THETA ** (-jnp.arange(0, rot_dim, 2, dtype=F32) / rot_dim)
    ang = jnp.arange(seq, dtype=F32)[:, None] * inv[None, :]
    cos, sin = jnp.cos(ang), jnp.sin(ang)
    lane = jnp.arange(LANES) % period
    idx = jnp.where(lane < half, lane, jnp.clip(lane - half, 0, half - 1))
    first = (lane < half)[None, :]
    second = ((lane >= half) & (lane < 2 * half))[None, :]
    c = jnp.where(first | second, cos[:, idx], 1.0)
    s_up = jnp.where(first, -sin[:, idx], 0.0)
    s_dn = jnp.where(second, sin[:, idx], 0.0)
    return c.astype(F32), s_up.astype(F32), s_dn.astype(F32)


def _pad_cols(w, n):
    return jnp.pad(w, ((0, 0), (0, n - w.shape[1])))


def _attn_tile(s):
    return min(512, s)


def _mixer_dsa(x, mod, w_in, kv_norm, w_kv_up, tabs, tabs_i, *, tm):
    bsz, s, d = x.shape
    t = _attn_tile(s)
    k_sel = min(A_TOPK_MAX, s // 4)
    n_qi = A_IDX_HEADS * A_IDX_DIM
    c0 = d + A_KV_RANK
    w_ki = w_in[:, c0 + n_qi:c0 + n_qi + A_IDX_DIM]
    w_wi = w_in[:, c0 + n_qi + A_IDX_DIM:]
    w_cat = jnp.concatenate([w_in[:, :c0 + n_qi], jnp.tile(w_ki, (1, n_qi // A_IDX_DIM)),
                             _pad_cols(w_wi, LANES)], axis=1).astype(BF16)
    wk, wv = w_kv_up[:, :HEAD_DIM], w_kv_up[:, HEAD_DIM:]
    wkv = jnp.concatenate([wk, wk, wv, wv], axis=1).astype(BF16)
    q, k2, v2, qi, ki, wi = _proj_call(
        functools.partial(_dsa_proj_kernel, d=d), x, mod,
        [w_cat, kv_norm.reshape(1, -1), wkv], list(tabs) + list(tabs_i),
        [(d, BF16), (LANES, BF16), (LANES, BF16), (n_qi, BF16), (n_qi, BF16), (LANES, F32)],
        tm=tm, name="dsa_in_proj")
    nt = s // t
    return pl.pallas_call(
        functools.partial(_dsa_attn_kernel, t=t, k_sel=k_sel),
        out_shape=jax.ShapeDtypeStruct((bsz, s, d), BF16),
        grid=(bsz, nt, d // LANES),
        in_specs=[pl.BlockSpec((1, t, LANES), lambda b, i, h: (b, i, h)),
                  pl.BlockSpec((1, s, LANES), lambda b, i, h: (b, 0, 0)),
                  pl.BlockSpec((1, s, LANES), lambda b, i, h: (b, 0, 0)),
                  pl.BlockSpec((1, t, n_qi), lambda b, i, h: (b, i, 0)),
                  pl.BlockSpec((1, s, n_qi), lambda b, i, h: (b, 0, 0)),
                  pl.BlockSpec((1, t, LANES), lambda b, i, h: (b, i, 0))],
        out_specs=pl.BlockSpec((1, t, LANES), lambda b, i, h: (b, i, h)),
        scratch_shapes=[pltpu.VMEM((nt, t, t), jnp.int32), pltpu.VMEM((nt, t, t), F32)],
        compiler_params=_cparams(3),
        name="dsa_attention",
    )(q, k2, v2, qi, ki, wi)


def _mixer_fox(x, mod, w_in, f_bias, *, tm):
    bsz, s, d = x.shape
    t = _attn_tile(s)
    nh = d // HEAD_DIM
    w_cat = _pad_cols(w_in, 3 * d + LANES).astype(BF16)
    fb = _pad_cols(f_bias.reshape(1, -1), LANES)
    q, k, v, cum = _proj_call(
        functools.partial(_fox_proj_kernel, d=d, tm=tm), x, mod, [w_cat, fb], [],
        [(d, BF16), (d, BF16), (d, BF16), (LANES, F32)],
        tm=tm, scratch=[pltpu.VMEM((1, LANES), F32)], name="fox_in_proj")
    nt = s // t
    cum = cum[:, :, :nh].reshape(bsz, s, nh // 2, 2)
    cq = jnp.transpose(cum, (0, 2, 1, 3))
    ck = jnp.transpose(cum.reshape(bsz, nt, t, nh // 2, 2), (0, 3, 1, 4, 2))
    return pl.pallas_call(
        functools.partial(_fox_attn_kernel, t=t),
        out_shape=jax.ShapeDtypeStruct((bsz, s, d), BF16),
        grid=(bsz, d // LANES, nt),
        in_specs=[pl.BlockSpec((1, t, LANES), lambda b, h, i: (b, i, h)),
                  pl.BlockSpec((1, s, LANES), lambda b, h, i: (b, 0, h)),
                  pl.BlockSpec((1, s, LANES), lambda b, h, i: (b, 0, h)),
                  pl.BlockSpec((1, 1, t, 2), lambda b, h, i: (b, h, i, 0)),
                  pl.BlockSpec((1, 1, nt, 2, t), lambda b, h, i: (b, h, 0, 0, 0))],
        out_specs=pl.BlockSpec((1, t, LANES), lambda b, h, i: (b, i, h)),
        compiler_params=_cparams(3),
        name="fox_attention",
    )(q, k, v, cq, ck)


def _mixer_swa(x, mod, w_in, sinks, tabs, *, tm):
    bsz, s, d = x.shape
    t = _attn_tile(s)
    n_kv = (w_in.shape[1] - d) // (2 * HEAD_DIM)
    pairs_per_kv = (d // LANES) // n_kv
    wq, wk, wv = w_in[:, :d], w_in[:, d:d + n_kv * HEAD_DIM], w_in[:, d + n_kv * HEAD_DIM:]

    def dup(w):
        w = w.reshape(w.shape[0], n_kv, 1, HEAD_DIM)
        return jnp.broadcast_to(w, (w.shape[0], n_kv, 2, HEAD_DIM)).reshape(w.shape[0], n_kv * LANES)

    w_cat = jnp.concatenate([wq, dup(wk), dup(wv)], axis=1).astype(BF16)
    q, k2, v2 = _proj_call(
        functools.partial(_swa_proj_kernel, d=d), x, mod, [w_cat], list(tabs),
        [(d, BF16), (n_kv * LANES, BF16), (n_kv * LANES, BF16)], tm=tm, name="swa_in_proj")
    return pl.pallas_call(
        functools.partial(_swa_attn_kernel, t=t, blk=C_WINDOW),
        out_shape=jax.ShapeDtypeStruct((bsz, s, d), BF16),
        grid=(bsz, d // LANES, s // t),
        in_specs=[pl.BlockSpec(memory_space=pltpu.SMEM),
                  pl.BlockSpec((1, t, LANES), lambda b, h, i: (b, i, h)),
                  pl.BlockSpec((1, s, LANES), lambda b, h, i: (b, 0, h // pairs_per_kv)),
                  pl.BlockSpec((1, s, LANES), lambda b, h, i: (b, 0, h // pairs_per_kv))],
        out_specs=pl.BlockSpec((1, t, LANES), lambda b, h, i: (b, i, h)),
        compiler_params=_cparams(3),
        name="swa_attention",
    )(sinks.astype(F32), q, k2, v2)


def _mixer_diff(x, mod, w_in, lam, subln_g, tabs, lambda_init, *, tm):
    bsz, s, d = x.shape
    t = _attn_tile(s)
    q, k, v = _proj_call(
        functools.partial(_diff_proj_kernel, d=d), x, mod, [w_in.astype(BF16)], list(tabs),
        [(d, BF16), (d, BF16), (d, BF16)], tm=tm, name="diff_in_proj")
    return pl.pallas_call(
        functools.partial(_diff_attn_kernel, t=t, lambda_init=lambda_init),
        out_shape=jax.ShapeDtypeStruct((bsz, s, d), BF16),
        grid=(bsz, d // LANES, s // t),
        in_specs=[pl.BlockSpec((1, t, LANES), lambda b, h, i: (b, i, h)),
                  pl.BlockSpec((1, s, LANES), lambda b, h, i: (b, 0, h)),
                  pl.BlockSpec((1, s, LANES), lambda b, h, i: (b, 0, h)),
                  pl.BlockSpec(lam.shape, lambda b, h, i: (0, 0)),
                  pl.BlockSpec((1, LANES), lambda b, h, i: (0, 0))],
        out_specs=pl.BlockSpec((1, t, LANES), lambda b, h, i: (b, i, h)),
        compiler_params=_cparams(3),
        name="diff_attention",
    )(q, k, v, lam.astype(F32), subln_g.reshape(1, -1))


def kernel(x, c, ln_g, ln_b, w_ada, b_ada, w_ffn_in, w_ffn_out, dsa_w_in, dsa_kv_norm, dsa_w_kv_up, dsa_w_out, fox_w_in, fox_f_bias, fox_w_out, swa_w_in, swa_sinks, swa_w_out, diff_w_in, diff_lambda, diff_subln, diff_w_out):
    bsz, s, d = x.shape
    depth = w_ada.shape[0]
    n_mixers = 4
    alpha = (2 * depth) ** 0.25
    tm = min(512, s)
    tabs = _rope_lane_tables(s, ROT_DIM, HEAD_DIM)
    tabs_i = _rope_lane_tables(s, A_IDX_DIM // 4, A_IDX_DIM)
    mods = _ada_call(c, w_ada, b_ada).reshape(depth, bsz, N_ADA, d)
    for i in range(depth):
        m, r = i % n_mixers, i // n_mixers
        mod = mods[i]
        x = _ffn_call(x, mod, w_ffn_in[i, 0].astype(BF16), w_ffn_out[i, 0].astype(BF16),
                      ln_g[i, 0], ln_b[i, 0], j=0, alpha=alpha)
        if m == 0:
            a = _mixer_dsa(x, mod, dsa_w_in[r], dsa_kv_norm[r], dsa_w_kv_up[r], tabs, tabs_i, tm=tm)
            w_out = dsa_w_out[r]
        elif m == 1:
            a = _mixer_fox(x, mod, fox_w_in[r], fox_f_bias[r], tm=tm)
            w_out = fox_w_out[r]
        elif m == 2:
            a = _mixer_swa(x, mod, swa_w_in[r], swa_sinks[r], tabs, tm=tm)
            w_out = swa_w_out[r]
        else:
            lambda_init = 0.8 - 0.6 * math.exp(-0.3 * i)
            a = _mixer_diff(x, mod, diff_w_in[r], diff_lambda[r], diff_subln[r], tabs, lambda_init, tm=tm)
            w_out = diff_w_out[r]
        x = _mixer_out_ffn_call(a, x, mod, w_out.astype(BF16), ln_g[i, 1], ln_b[i, 1],
                                w_ffn_in[i, 1].astype(BF16), w_ffn_out[i, 1].astype(BF16),
                                ln_g[i, 2], ln_b[i, 2], alpha=alpha)
    return x
```

```python
import functools
import math

import jax
import jax.numpy as jnp
from jax import lax
from jax.experimental import pallas as pl
from jax.experimental.pallas import tpu as pltpu

F32 = jnp.float32
BF16 = jnp.bfloat16

HEAD_DIM = 64
ROT_DIM = HEAD_DIM // 4
ROPE_THETA = 500000.0
LN_EPS = 1e-5
RMS_EPS = 1e-6
N_ADA = 9
A_KV_RANK = 128
A_IDX_HEADS = 8
A_IDX_DIM = 32
A_TOPK_MAX = 256
C_WINDOW = 128
LANES = 128
NEG = -1e30
INT_MIN = -(2 ** 31)
LOG2E = math.log2(math.e)
Q_SCALE = HEAD_DIM ** -0.5 * LOG2E
VMEM_LIMIT = 56 * 1024 * 1024


def _cparams(n_axes):
    return pltpu.CompilerParams(dimension_semantics=("arbitrary",) * n_axes,
                                vmem_limit_bytes=VMEM_LIMIT)


def _resident(shape):
    nd = len(shape)
    return pl.BlockSpec(shape, lambda *_: (0,) * nd, pipeline_mode=pl.Buffered(1))


def _resident_layer(stacked, lead):
    rows, cols = stacked.shape[-2:]
    return pl.BlockSpec((None,) * len(lead) + (rows, cols), lambda *_: tuple(lead) + (0, 0),
                        pipeline_mode=pl.Buffered(1))


def _dot(a, b):
    return jnp.dot(a, b, preferred_element_type=F32)


def _dot_nt(a, b):
    return lax.dot_general(a, b, (((1,), (1,)), ((), ())), preferred_element_type=F32)


def _sigmoid(x):
    return 1.0 / (1.0 + jnp.exp(-x))


def _layer_norm(z, g, b):
    mu = jnp.mean(z, axis=-1, keepdims=True)
    zc = z - mu
    var = jnp.mean(zc * zc, axis=-1, keepdims=True)
    return zc * lax.rsqrt(var + LN_EPS) * g + b


def _modulate(x, mod_ref, j):
    shift = mod_ref[0, 3 * j:3 * j + 1, :]
    scale = mod_ref[0, 3 * j + 1:3 * j + 2, :]
    return (x * (1.0 + scale) + shift).astype(BF16)


def _rope(xs, c, s_up, s_dn, half):
    return (xs * c + pltpu.roll(xs, LANES - half, 1) * s_up + pltpu.roll(xs, half, 1) * s_dn)


def _split_bf16(x):
    hi = x.astype(BF16)
    return hi, (x - hi.astype(F32)).astype(BF16)


def _ada_kernel(c_ref, w_ref, b_ref, o_ref):
    c = c_ref[...]
    c_hi, c_lo = _split_bf16(c * _sigmoid(c))
    w_hi, w_lo = _split_bf16(w_ref[0])
    o_ref[0] = _dot(c_hi, w_hi) + (_dot(c_hi, w_lo) + _dot(c_lo, w_hi)) + b_ref[0]


def _ada_call(c, w_ada, b_ada):
    depth, d, nd = w_ada.shape
    b = c.shape[0]
    n_blk = nd // d
    return pl.pallas_call(
        _ada_kernel,
        out_shape=jax.ShapeDtypeStruct((depth, b, nd), F32),
        grid=(depth, n_blk),
        in_specs=[pl.BlockSpec((b, d), lambda l, n: (0, 0)),
                  pl.BlockSpec((1, d, d), lambda l, n: (l, 0, n)),
                  pl.BlockSpec((1, 1, d), lambda l, n: (l, 0, n))],
        out_specs=pl.BlockSpec((1, b, d), lambda l, n: (l, 0, n)),
        compiler_params=_cparams(2),
        name="ada_mod",
    )(c, w_ada, b_ada.reshape(depth, 1, nd))


FFN_ROWS = 1024
FFN_SUB = 512
FFN_CHUNK = 256


def _swiglu_rows(h, win_ref, wout_ref, dff):
    y = jnp.zeros((h.shape[0], wout_ref.shape[1]), F32)
    a_prev = None
    for c0 in range(0, dff, FFN_CHUNK):
        g = _dot(h, win_ref[:, c0:c0 + FFN_CHUNK])
        u = _dot(h, win_ref[:, dff + c0:dff + c0 + FFN_CHUNK])
        if a_prev is not None:
            y = y + _dot(a_prev, wout_ref[c0 - FFN_CHUNK:c0, :])
        a_prev = (g * _sigmoid(g) * u).astype(BF16)
    return y + _dot(a_prev, wout_ref[dff - FFN_CHUNK:dff, :])


def _ffn_kernel(x_ref, mod_ref, win_ref, wout_ref, g_ref, b_ref, o_ref, *, j, dff, alpha):
    gate = mod_ref[0, 3 * j + 2:3 * j + 3, :]
    for r0 in range(0, x_ref.shape[1], FFN_SUB):
        x = x_ref[0, r0:r0 + FFN_SUB, :]
        y = _swiglu_rows(_modulate(x, mod_ref, j), win_ref, wout_ref, dff)
        z = alpha * x + 0.5 * (1.0 + gate) * y
        o_ref[0, r0:r0 + FFN_SUB, :] = _layer_norm(z, g_ref[...], b_ref[...])


def _mixer_out_ffn_kernel(a_ref, x_ref, mod_ref, wo_ref, g1_ref, b1_ref, win_ref, wout_ref, g2_ref, b2_ref,
                          o_ref, *, dff, alpha):
    subs = range(0, x_ref.shape[1], FFN_SUB)
    ys = [_dot(a_ref[0, r0:r0 + FFN_SUB, :], wo_ref[...]) for r0 in subs]
    gate1 = mod_ref[0, 5:6, :]
    gate2 = mod_ref[0, 8:9, :]
    for r0, y1 in zip(subs, ys):
        x = _layer_norm(alpha * x_ref[0, r0:r0 + FFN_SUB, :] + (1.0 + gate1) * y1, g1_ref[...], b1_ref[...])
        y = _swiglu_rows(_modulate(x, mod_ref, 2), win_ref, wout_ref, dff)
        z = alpha * x + 0.5 * (1.0 + gate2) * y
        o_ref[0, r0:r0 + FFN_SUB, :] = _layer_norm(z, g2_ref[...], b2_ref[...])


def _ffn_rows(s):
    return min(FFN_ROWS, s)


def _ffn_call(x, mod, w_in, w_out, lead, g, b, *, j, alpha):
    bsz, s, d = x.shape
    dff = w_out.shape[-2]
    tm = _ffn_rows(s)
    return pl.pallas_call(
        functools.partial(_ffn_kernel, j=j, dff=dff, alpha=alpha),
        out_shape=jax.ShapeDtypeStruct(x.shape, F32),
        grid=(bsz, s // tm),
        in_specs=[pl.BlockSpec((1, tm, d), lambda bi, si: (bi, si, 0)),
                  pl.BlockSpec((1, N_ADA, d), lambda bi, si: (bi, 0, 0)),
                  _resident_layer(w_in, lead), _resident_layer(w_out, lead),
                  _resident((1, d)), _resident((1, d))],
        out_specs=pl.BlockSpec((1, tm, d), lambda bi, si: (bi, si, 0)),
        compiler_params=_cparams(2),
        name="ffn_sublayer",
    )(x, mod, w_in, w_out, g.reshape(1, d), b.reshape(1, d))


def _mixer_out_ffn_call(a, x, mod, w_o, g1, b1, w_in, w_out, lead, g2, b2, *, alpha):
    bsz, s, d = x.shape
    dff = w_out.shape[-2]
    tm = _ffn_rows(s)
    row_block = lambda width: pl.BlockSpec((1, tm, width), lambda bi, si: (bi, si, 0))
    return pl.pallas_call(
        functools.partial(_mixer_out_ffn_kernel, dff=dff, alpha=alpha),
        out_shape=jax.ShapeDtypeStruct(x.shape, F32),
        grid=(bsz, s // tm),
        in_specs=[row_block(a.shape[-1]), row_block(d),
                  pl.BlockSpec((1, N_ADA, d), lambda bi, si: (bi, 0, 0)),
                  _resident(w_o.shape), _resident((1, d)), _resident((1, d)),
                  _resident_layer(w_in, lead), _resident_layer(w_out, lead),
                  _resident((1, d)), _resident((1, d))],
        out_specs=row_block(d),
        compiler_params=_cparams(2),
        name="mixer_out_ffn",
    )(a, x, mod, w_o, g1.reshape(1, d), b1.reshape(1, d), w_in, w_out, g2.reshape(1, d), b2.reshape(1, d))


def _rope_store(dst_ref, p, col0, n_blk, tabs, half, scale=None):
    c, s_up, s_dn = tabs
    for i in range(n_blk):
        xs = _rope(p[:, col0 + i * LANES:col0 + (i + 1) * LANES], c, s_up, s_dn, half)
        if scale is not None:
            xs = xs * scale
        dst_ref[0, :, i * LANES:(i + 1) * LANES] = xs.astype(dst_ref.dtype)


def _plain_store(dst_ref, p, col0, n_blk):
    dst_ref[0] = p[:, col0:col0 + n_blk * LANES].astype(dst_ref.dtype)


def _dsa_proj_kernel(x_ref, mod_ref, w_ref, kvn_ref, wkv_ref, c_ref, su_ref, sd_ref,
                     ci_ref, sui_ref, sdi_ref,
                     q_ref, k_ref, v_ref, qi_ref, ki_ref, wi_ref, *, d):
    h = _modulate(x_ref[0], mod_ref, 1)
    p = _dot(h, w_ref[...])
    tabs = (c_ref[...], su_ref[...], sd_ref[...])
    tabs_i = (ci_ref[...], sui_ref[...], sdi_ref[...])
    nq = d // LANES
    _rope_store(q_ref, p, 0, nq, tabs, ROT_DIM // 2, scale=Q_SCALE)
    ckv = p[:, d:d + A_KV_RANK]
    ckv = ckv * lax.rsqrt(jnp.mean(ckv * ckv, axis=-1, keepdims=True) + RMS_EPS) * kvn_ref[...]
    kv = _dot(ckv.astype(BF16), wkv_ref[...])
    _rope_store(k_ref, kv, 0, 1, tabs, ROT_DIM // 2)
    _plain_store(v_ref, kv, LANES, 1)
    c0 = d + A_KV_RANK
    _rope_store(qi_ref, p, c0, 2, tabs_i, A_IDX_DIM // 8)
    _rope_store(ki_ref, p, c0 + 2 * LANES, 2, tabs_i, A_IDX_DIM // 8)
    wi_ref[0] = p[:, c0 + 4 * LANES:c0 + 5 * LANES] * (A_IDX_HEADS ** -0.5 * A_IDX_DIM ** -0.5)


def _fox_proj_kernel(x_ref, mod_ref, w_ref, fb_ref, q_ref, k_ref, v_ref, cum_ref, carry_ref, *, d, tm):
    h = _modulate(x_ref[0], mod_ref, 1)
    p = _dot(h, w_ref[...])
    q_ref[0] = (p[:, 0:d] * Q_SCALE).astype(BF16)
    k_ref[0] = p[:, d:2 * d].astype(BF16)
    v_ref[0] = p[:, 2 * d:3 * d].astype(BF16)
    f = p[:, 3 * d:3 * d + LANES] + fb_ref[...]
    logf = jnp.minimum(f, 0.0) - jnp.log1p(jnp.exp(-jnp.abs(f)))

    @pl.when(pl.program_id(1) == 0)
    def _():
        carry_ref[...] = jnp.zeros_like(carry_ref)

    hi = logf.astype(BF16)
    r1 = logf - hi.astype(F32)
    mid = r1.astype(BF16)
    lo = (r1 - mid.astype(F32)).astype(BF16)
    row = lax.broadcasted_iota(jnp.int32, (tm, tm), 0)
    col = lax.broadcasted_iota(jnp.int32, (tm, tm), 1)
    tri = (row >= col).astype(BF16)
    cum = _dot(tri, hi) + _dot(tri, mid) + _dot(tri, lo) + carry_ref[...]
    cum_ref[0] = cum * LOG2E
    carry_ref[...] = cum[tm - 1:tm, :]


def _swa_proj_kernel(x_ref, mod_ref, w_ref, c_ref, su_ref, sd_ref, q_ref, k_ref, v_ref, *, d):
    h = _modulate(x_ref[0], mod_ref, 1)
    p = _dot(h, w_ref[...])
    tabs = (c_ref[...], su_ref[...], sd_ref[...])
    nq = d // LANES
    _rope_store(q_ref, p, 0, nq, tabs, ROT_DIM // 2, scale=Q_SCALE)
    _rope_store(k_ref, p, d, 2, tabs, ROT_DIM // 2)
    _plain_store(v_ref, p, d + 2 * LANES, 2)


def _diff_proj_kernel(x_ref, mod_ref, w_ref, c_ref, su_ref, sd_ref, q_ref, k_ref, v_ref, *, d):
    h = _modulate(x_ref[0], mod_ref, 1)
    p = _dot(h, w_ref[...])
    tabs = (c_ref[...], su_ref[...], sd_ref[...])
    nq = d // LANES
    _rope_store(q_ref, p, 0, nq, tabs, ROT_DIM // 2, scale=Q_SCALE)
    _rope_store(k_ref, p, d, nq, tabs, ROT_DIM // 2)
    _plain_store(v_ref, p, 2 * d, nq)


def _proj_call(kern, x, mod, consts, tabs, outs, *, tm, scratch=(), name):
    bsz, s, d = x.shape
    in_specs = [pl.BlockSpec((1, tm, d), lambda bi, si: (bi, si, 0)),
                pl.BlockSpec((1, N_ADA, d), lambda bi, si: (bi, 0, 0))]
    in_specs += [_resident(a.shape) for a in consts]
    in_specs += [pl.BlockSpec((tm, LANES), lambda bi, si: (si, 0)) for _ in tabs]
    return pl.pallas_call(
        kern,
        out_shape=[jax.ShapeDtypeStruct((bsz, s, w), dt) for w, dt in outs],
        grid=(bsz, s // tm),
        in_specs=in_specs,
        out_specs=[pl.BlockSpec((1, tm, w), lambda bi, si: (bi, si, 0)) for w, _ in outs],
        scratch_shapes=list(scratch),
        compiler_params=_cparams(2),
        name=name,
    )(x, mod, *consts, *tabs)


def _lane_is_low(shape):
    return lax.broadcasted_iota(jnp.int32, shape, 1) < HEAD_DIM


def _split_pair(q):
    low = _lane_is_low(q.shape)
    zero = jnp.zeros_like(q)
    return jnp.where(low, q, zero), jnp.where(low, zero, q)


def _lane_tile(x, width):
    return jnp.concatenate([x] * (width // LANES), axis=1)


def _softmax_tile(s, m, row_term=None):
    tile_max = jnp.broadcast_to(jnp.max(s, axis=-1, keepdims=True), m.shape)
    if row_term is not None:
        tile_max = tile_max + row_term
    m_new = jnp.maximum(m, tile_max)
    shift = m_new if row_term is None else m_new - row_term
    p = jnp.exp2(s - _lane_tile(shift, s.shape[1])).astype(BF16)
    return p, jnp.exp2(m - m_new), m_new


def _causal_mask(n):
    row = lax.broadcasted_iota(jnp.int32, (n, n), 0)
    col = lax.broadcasted_iota(jnp.int32, (n, n), 1)
    return row >= col


FLASH_BLOCK = 8


def _flash_tiles(n_plain, max_tiles, logits, consume, state, last_masked=True):
    def run_block(first, count, mask_last, st):
        s_prev = logits(first, mask_last and count == 1)
        for k in range(1, count):
            s_next = logits(first + k, mask_last and k == count - 1)
            st = consume(first + k - 1, s_prev, st)
            s_prev = s_next
        return consume(first + count - 1, s_prev, st)

    first = 0
    if max_tiles > FLASH_BLOCK:
        n_full = n_plain // FLASH_BLOCK
        state = lax.fori_loop(0, n_full, lambda j, st: run_block(FLASH_BLOCK * j, FLASH_BLOCK, False, st), state)
        first = n_full * FLASH_BLOCK
    branches = [functools.partial(run_block, first, c, last_masked)
                for c in range(1, min(FLASH_BLOCK, max_tiles) + 1)]
    return lax.switch(n_plain - first, branches, state)


def _pair_values(vj):
    low = _lane_is_low(vj.shape)
    one = jnp.ones_like(vj)
    return jnp.where(low, vj, one), jnp.where(low, one, vj)


def _pair_init(t):
    m = jnp.full((t, LANES), NEG, F32)
    zacc = jnp.zeros((t, LANES), F32)
    return m, m, zacc, zacc


def _pair_consume(s, state, v0, v1, row_terms=(None, None)):
    m0, m1, acc0, acc1 = state
    p0, a0, m0 = _softmax_tile(s[0], m0, row_terms[0])
    p1, a1, m1 = _softmax_tile(s[1], m1, row_terms[1])
    return m0, m1, a0 * acc0 + _dot(p0, v0), a1 * acc1 + _dot(p1, v1)


def _pair_finish(acc0, acc1):
    low = _lane_is_low(acc0.shape)
    return jnp.where(low, acc0 / pltpu.roll(acc0, HEAD_DIM, 1), acc1 / pltpu.roll(acc1, HEAD_DIM, 1))


def _key_tile(ref, j, t):
    return ref[0, pl.ds(pl.multiple_of(j * t, t), t), :]


def _fox_attn_kernel(q_ref, k_ref, v_ref, ck_ref, o_ref, *, t):
    i = pl.program_id(2)
    q0, q1 = _split_pair(q_ref[0])
    cq = ck_ref[0, 0, i]
    cq0 = jnp.transpose(jnp.broadcast_to(cq[0:1, :], (LANES, t)))
    cq1 = jnp.transpose(jnp.broadcast_to(cq[1:2, :], (LANES, t)))

    def logits(j, masked):
        kj = _key_tile(k_ref, j, t)
        ck = ck_ref[0, 0, j]
        s0 = _dot_nt(q0, kj) - ck[0:1, :]
        s1 = _dot_nt(q1, kj) - ck[1:2, :]
        if masked:
            ok = _causal_mask(t)
            s0 = jnp.where(ok, s0, NEG)
            s1 = jnp.where(ok, s1, NEG)
        return s0, s1

    def consume(j, s, state):
        v0, v1 = _pair_values(_key_tile(v_ref, j, t))
        return _pair_consume(s, state, v0, v1, (cq0, cq1))

    _, _, acc0, acc1 = _flash_tiles(i, k_ref.shape[1] // t, logits, consume, _pair_init(t))
    o_ref[0] = _pair_finish(acc0, acc1).astype(o_ref.dtype)


def _diff_attn_kernel(q_ref, k_ref, v_ref, lam_ref, g_ref, o_ref, *, t, lambda_init):
    i = pl.program_id(2)
    q0, q1 = _split_pair(q_ref[0])

    def logits(j, masked):
        kj = _key_tile(k_ref, j, t)
        s0 = _dot_nt(q0, kj)
        s1 = _dot_nt(q1, kj)
        if masked:
            ok = _causal_mask(t)
            s0 = jnp.where(ok, s0, NEG)
            s1 = jnp.where(ok, s1, NEG)
        return s0, s1

    def consume(j, s, state):
        m0, m1, acc0, acc1 = state
        vj = _key_tile(v_ref, j, t)
        v_ext = jnp.concatenate([vj, jnp.ones_like(vj)], axis=1)
        p0, a0, m0 = _softmax_tile(s[0], m0)
        p1, a1, m1 = _softmax_tile(s[1], m1)
        acc0 = _lane_tile(a0, 2 * LANES) * acc0 + _dot(p0, v_ext)
        acc1 = _lane_tile(a1, 2 * LANES) * acc1 + _dot(p1, v_ext)
        return m0, m1, acc0, acc1

    m_init = jnp.full((t, LANES), NEG, F32)
    zacc = jnp.zeros((t, 2 * LANES), F32)
    _, _, acc0, acc1 = _flash_tiles(i, k_ref.shape[1] // t, logits, consume, (m_init, m_init, zacc, zacc))
    lam = lam_ref[...]
    lam_val = (jnp.exp(jnp.sum(lam[0:1] * lam[1:2], axis=-1, keepdims=True))
               - jnp.exp(jnp.sum(lam[2:3] * lam[3:4], axis=-1, keepdims=True)) + lambda_init)
    out = acc0[:, :LANES] / acc0[:, LANES:] - lam_val * (acc1[:, :LANES] / acc1[:, LANES:])
    out = out * lax.rsqrt(jnp.mean(out * out, axis=-1, keepdims=True) + RMS_EPS) * g_ref[...]
    o_ref[0] = (out * (1.0 - lambda_init)).astype(o_ref.dtype)


def _swa_attn_kernel(sink_ref, q_ref, k_ref, v_ref, o_ref, *, t, blk):
    hp = pl.program_id(1)
    i = pl.program_id(2)
    low = _lane_is_low((blk, LANES))
    sinks = (sink_ref[2 * hp] * LOG2E, sink_ref[2 * hp + 1] * LOG2E)
    row = lax.broadcasted_iota(jnp.int32, (blk, 2 * blk), 0)
    col = lax.broadcasted_iota(jnp.int32, (blk, 2 * blk), 1)
    pending = []
    for u in range(t // blk):
        r0 = i * t + u * blk
        k0 = pl.multiple_of(jnp.maximum(r0 - blk, 0), blk)
        kj = k_ref[0, pl.ds(k0, 2 * blk), :]
        dist = (r0 + row) - (k0 + col)
        ok = (dist >= 0) & (dist < C_WINDOW)
        logits = [jnp.where(ok, _dot_nt(qe, kj), NEG) for qe in _split_pair(q_ref[0, u * blk:(u + 1) * blk, :])]
        pending.append((u, k0, logits))
    for u, k0, logits in pending:
        vj = v_ref[0, pl.ds(k0, 2 * blk), :]
        outs = []
        for s, sink in zip(logits, sinks):
            m = jnp.maximum(jnp.max(s, axis=-1, keepdims=True), sink)
            p = jnp.exp2(s - m)
            denom = jnp.sum(p, axis=-1, keepdims=True) + jnp.exp2(sink - m)
            outs.append(_dot(p.astype(BF16), vj) / denom)
        o_ref[0, u * blk:(u + 1) * blk, :] = jnp.where(low, outs[0], outs[1]).astype(o_ref.dtype)


def _sortable(score):
    bits = lax.bitcast_convert_type(score, jnp.int32)
    return bits ^ (lax.shift_right_arithmetic(bits, 31) & jnp.int32(0x7FFFFFFF))


def _sublane_fold(x):
    return jnp.sum(x.reshape(x.shape[0] // 8, 8, x.shape[1]), axis=0)


def _dsa_attn_kernel(q_ref, k_ref, v_ref, qi_ref, ki_ref, wi_ref, o_ref, keys_ref, selb_ref, *, t, k_sel):
    i = pl.program_id(1)
    hp = pl.program_id(2)

    def count_keys(pred_fn):
        def body(j, acc):
            return acc + _sublane_fold(jnp.where(pred_fn(keys_ref[j]), 1.0, 0.0))
        acc = lax.fori_loop(0, i + 1, body, jnp.zeros((8, t), F32))
        return jnp.sum(acc, axis=0, keepdims=True)

    @pl.when(hp == 0)
    def _select():
        qi = qi_ref[0]
        lane = lax.broadcasted_iota(jnp.int32, qi.shape, 1)
        qis = [jnp.where((lane >= h * A_IDX_DIM) & (lane < (h + 1) * A_IDX_DIM), qi, jnp.zeros_like(qi))
               for h in range(A_IDX_HEADS)]
        w_t = jnp.transpose(wi_ref[0])
        key_pos = lax.broadcasted_iota(jnp.int32, (t, t), 0)
        qry_pos = lax.broadcasted_iota(jnp.int32, (t, t), 1)

        def score_tile(j, masked):
            kij = ki_ref[0, pl.ds(pl.multiple_of(j * t, t), t), :]
            score = jnp.zeros((t, t), F32)
            for h in range(A_IDX_HEADS):
                score = score + w_t[h:h + 1, :] * jnp.maximum(_dot_nt(kij, qis[h]), 0.0)
            key = _sortable(score)
            if masked:
                key = jnp.where(key_pos <= qry_pos, key, INT_MIN)
            keys_ref[j] = key

        def score_body(j, c):
            score_tile(j, False)
            return c
        lax.fori_loop(0, i, score_body, 0)
        score_tile(i, True)

        def bit_body(b, thr_u):
            cand = thr_u | lax.shift_left(jnp.int32(1), 31 - b)
            cand_s = cand ^ INT_MIN
            cnt = count_keys(lambda key: key >= cand_s)
            return jnp.where(cnt >= k_sel, cand, thr_u)
        thr_u = lax.fori_loop(0, 32, bit_body, jnp.zeros((1, t), jnp.int32))
        thr = thr_u ^ INT_MIN
        need = k_sel - count_keys(lambda key: key > thr)

        earlier = (qry_pos < key_pos).astype(BF16)

        def sel_tile(j, seen, masked):
            key = keys_ref[j]
            eq = key == thr
            eq_f = jnp.where(eq, 1.0, 0.0)
            rank = _dot(earlier, eq_f.astype(BF16)) + seen
            sel = (key > thr) | (eq & (rank < need))
            if masked:
                sel = sel & (key_pos <= qry_pos)
            selb_ref[j] = jnp.transpose(jnp.where(sel, 0.0, NEG))
            return seen + jnp.sum(_sublane_fold(eq_f), axis=0, keepdims=True)

        seen = lax.fori_loop(0, i, lambda j, c: sel_tile(j, c, False), jnp.zeros((1, t), F32))
        sel_tile(i, seen, True)

    q0, q1 = _split_pair(q_ref[0])

    def logits(j, _):
        kj = _key_tile(k_ref, j, t)
        bias = selb_ref[j]
        return _dot_nt(q0, kj) + bias, _dot_nt(q1, kj) + bias

    def consume(j, s, state):
        v0, v1 = _pair_values(_key_tile(v_ref, j, t))
        return _pair_consume(s, state, v0, v1)

    _, _, acc0, acc1 = _flash_tiles(i, k_ref.shape[1] // t, logits, consume, _pair_init(t))
    o_ref[0] = _pair_finish(acc0, acc1).astype(o_ref.dtype)


def _rope_lane_tables(seq, rot_dim, period):
    half = rot_dim // 2
    inv = ROPE_THETA ** (-jnp.arange(0, rot_dim, 2, dtype=F32) / rot_dim)
    ang = jnp.arange(seq, dtype=F32)[:, None] * inv[None, :]
    cos, sin = jnp.cos(ang), jnp.sin(ang)
    lane = jnp.arange(LANES) % period
    idx = jnp.where(lane < half, lane, jnp.clip(lane - half, 0, half - 1))
    first = (lane < half)[None, :]
    second = ((lane >= half) & (lane < 2 * half))[None, :]
    c = jnp.where(first | second, cos[:, idx], 1.0)
    s_up = jnp.where(first, -sin[:, idx], 0.0)
    s_dn = jnp.where(second, sin[:, idx], 0.0)
    return c.astype(F32), s_up.astype(F32), s_dn.astype(F32)


def _pad_cols(w, n):
    return jnp.pad(w, ((0, 0), (0, n - w.shape[1])))


def _attn_tile(s):
    return min(512, s)


def _mixer_dsa(x, mod, w_in, kv_norm, w_kv_up, tabs, tabs_i, *, tm):
    bsz, s, d = x.shape
    t = _attn_tile(s)
    k_sel = min(A_TOPK_MAX, s // 4)
    n_qi = A_IDX_HEADS * A_IDX_DIM
    c0 = d + A_KV_RANK
    w_ki = w_in[:, c0 + n_qi:c0 + n_qi + A_IDX_DIM]
    w_wi = w_in[:, c0 + n_qi + A_IDX_DIM:]
    w_cat = jnp.concatenate([w_in[:, :c0 + n_qi], jnp.tile(w_ki, (1, n_qi // A_IDX_DIM)),
                             _pad_cols(w_wi, LANES)], axis=1).astype(BF16)
    wk, wv = w_kv_up[:, :HEAD_DIM], w_kv_up[:, HEAD_DIM:]
    wkv = jnp.concatenate([wk, wk, wv, wv], axis=1).astype(BF16)
    q, k2, v2, qi, ki, wi = _proj_call(
        functools.partial(_dsa_proj_kernel, d=d), x, mod,
        [w_cat, kv_norm.reshape(1, -1), wkv], list(tabs) + list(tabs_i),
        [(d, BF16), (LANES, BF16), (LANES, BF16), (n_qi, BF16), (n_qi, BF16), (LANES, F32)],
        tm=tm, name="dsa_in_proj")
    nt = s // t
    return pl.pallas_call(
        functools.partial(_dsa_attn_kernel, t=t, k_sel=k_sel),
        out_shape=jax.ShapeDtypeStruct((bsz, s, d), BF16),
        grid=(bsz, nt, d // LANES),
        in_specs=[pl.BlockSpec((1, t, LANES), lambda b, i, h: (b, i, h)),
                  pl.BlockSpec((1, s, LANES), lambda b, i, h: (b, 0, 0)),
                  pl.BlockSpec((1, s, LANES), lambda b, i, h: (b, 0, 0)),
                  pl.BlockSpec((1, t, n_qi), lambda b, i, h: (b, i, 0)),
                  pl.BlockSpec((1, s, n_qi), lambda b, i, h: (b, 0, 0)),
                  pl.BlockSpec((1, t, LANES), lambda b, i, h: (b, i, 0))],
        out_specs=pl.BlockSpec((1, t, LANES), lambda b, i, h: (b, i, h)),
        scratch_shapes=[pltpu.VMEM((nt, t, t), jnp.int32), pltpu.VMEM((nt, t, t), F32)],
        compiler_params=_cparams(3),
        name="dsa_attention",
    )(q, k2, v2, qi, ki, wi)


def _mixer_fox(x, mod, w_in, f_bias, *, tm):
    bsz, s, d = x.shape
    t = _attn_tile(s)
    nh = d // HEAD_DIM
    w_cat = _pad_cols(w_in, 3 * d + LANES).astype(BF16)
    fb = _pad_cols(f_bias.reshape(1, -1), LANES)
    q, k, v, cum = _proj_call(
        functools.partial(_fox_proj_kernel, d=d, tm=tm), x, mod, [w_cat, fb], [],
        [(d, BF16), (d, BF16), (d, BF16), (LANES, F32)],
        tm=tm, scratch=[pltpu.VMEM((1, LANES), F32)], name="fox_in_proj")
    nt = s // t
    ck = jnp.transpose(cum[:, :, :nh].reshape(bsz, nt, t, nh // 2, 2), (0, 3, 1, 4, 2))
    return pl.pallas_call(
        functools.partial(_fox_attn_kernel, t=t),
        out_shape=jax.ShapeDtypeStruct((bsz, s, d), BF16),
        grid=(bsz, d // LANES, nt),
        in_specs=[pl.BlockSpec((1, t, LANES), lambda b, h, i: (b, i, h)),
                  pl.BlockSpec((1, s, LANES), lambda b, h, i: (b, 0, h)),
                  pl.BlockSpec((1, s, LANES), lambda b, h, i: (b, 0, h)),
                  pl.BlockSpec((1, 1, nt, 2, t), lambda b, h, i: (b, h, 0, 0, 0))],
        out_specs=pl.BlockSpec((1, t, LANES), lambda b, h, i: (b, i, h)),
        compiler_params=_cparams(3),
        name="fox_attention",
    )(q, k, v, ck)


def _mixer_swa(x, mod, w_in, sinks, tabs, *, tm):
    bsz, s, d = x.shape
    t = _attn_tile(s)
    n_kv = (w_in.shape[1] - d) // (2 * HEAD_DIM)
    pairs_per_kv = (d // LANES) // n_kv
    wq, wk, wv = w_in[:, :d], w_in[:, d:d + n_kv * HEAD_DIM], w_in[:, d + n_kv * HEAD_DIM:]

    def dup(w):
        w = w.reshape(w.shape[0], n_kv, 1, HEAD_DIM)
        return jnp.broadcast_to(w, (w.shape[0], n_kv, 2, HEAD_DIM)).reshape(w.shape[0], n_kv * LANES)

    w_cat = jnp.concatenate([wq, dup(wk), dup(wv)], axis=1).astype(BF16)
    q, k2, v2 = _proj_call(
        functools.partial(_swa_proj_kernel, d=d), x, mod, [w_cat], list(tabs),
        [(d, BF16), (n_kv * LANES, BF16), (n_kv * LANES, BF16)], tm=tm, name="swa_in_proj")
    return pl.pallas_call(
        functools.partial(_swa_attn_kernel, t=t, blk=C_WINDOW),
        out_shape=jax.ShapeDtypeStruct((bsz, s, d), BF16),
        grid=(bsz, d // LANES, s // t),
        in_specs=[pl.BlockSpec(memory_space=pltpu.SMEM),
                  pl.BlockSpec((1, t, LANES), lambda b, h, i: (b, i, h)),
                  pl.BlockSpec((1, s, LANES), lambda b, h, i: (b, 0, h // pairs_per_kv)),
                  pl.BlockSpec((1, s, LANES), lambda b, h, i: (b, 0, h // pairs_per_kv))],
        out_specs=pl.BlockSpec((1, t, LANES), lambda b, h, i: (b, i, h)),
        compiler_params=_cparams(3),
        name="swa_attention",
    )(sinks.astype(F32), q, k2, v2)


def _mixer_diff(x, mod, w_in, lam, subln_g, tabs, lambda_init, *, tm):
    bsz, s, d = x.shape
    t = _attn_tile(s)
    q, k, v = _proj_call(
        functools.partial(_diff_proj_kernel, d=d), x, mod, [w_in.astype(BF16)], list(tabs),
        [(d, BF16), (d, BF16), (d, BF16)], tm=tm, name="diff_in_proj")
    return pl.pallas_call(
        functools.partial(_diff_attn_kernel, t=t, lambda_init=lambda_init),
        out_shape=jax.ShapeDtypeStruct((bsz, s, d), BF16),
        grid=(bsz, d // LANES, s // t),
        in_specs=[pl.BlockSpec((1, t, LANES), lambda b, h, i: (b, i, h)),
                  pl.BlockSpec((1, s, LANES), lambda b, h, i: (b, 0, h)),
                  pl.BlockSpec((1, s, LANES), lambda b, h, i: (b, 0, h)),
                  pl.BlockSpec(lam.shape, lambda b, h, i: (0, 0)),
                  pl.BlockSpec((1, LANES), lambda b, h, i: (0, 0))],
        out_specs=pl.BlockSpec((1, t, LANES), lambda b, h, i: (b, i, h)),
        compiler_params=_cparams(3),
        name="diff_attention",
    )(q, k, v, lam.astype(F32), subln_g.reshape(1, -1))


def kernel(x, c, ln_g, ln_b, w_ada, b_ada, w_ffn_in, w_ffn_out, dsa_w_in, dsa_kv_norm, dsa_w_kv_up, dsa_w_out, fox_w_in, fox_f_bias, fox_w_out, swa_w_in, swa_sinks, swa_w_out, diff_w_in, diff_lambda, diff_subln, diff_w_out):
    bsz, s, d = x.shape
    depth = w_ada.shape[0]
    n_mixers = 4
    alpha = (2 * depth) ** 0.25
    tm = min(512, s)
    tabs = _rope_lane_tables(s, ROT_DIM, HEAD_DIM)
    tabs_i = _rope_lane_tables(s, A_IDX_DIM // 4, A_IDX_DIM)
    mods = _ada_call(c, w_ada, b_ada).reshape(depth, bsz, N_ADA, d)
    w_in_bf, w_out_bf = w_ffn_in.astype(BF16), w_ffn_out.astype(BF16)
    for i in range(depth):
        m, r = i % n_mixers, i // n_mixers
        mod = mods[i]
        x = _ffn_call(x, mod, w_in_bf, w_out_bf, (i, 0), ln_g[i, 0], ln_b[i, 0], j=0, alpha=alpha)
        if m == 0:
            a = _mixer_dsa(x, mod, dsa_w_in[r], dsa_kv_norm[r], dsa_w_kv_up[r], tabs, tabs_i, tm=tm)
            w_out = dsa_w_out[r]
        elif m == 1:
            a = _mixer_fox(x, mod, fox_w_in[r], fox_f_bias[r], tm=tm)
            w_out = fox_w_out[r]
        elif m == 2:
            a = _mixer_swa(x, mod, swa_w_in[r], swa_sinks[r], tabs, tm=tm)
            w_out = swa_w_out[r]
        else:
            lambda_init = 0.8 - 0.6 * math.exp(-0.3 * i)
            a = _mixer_diff(x, mod, diff_w_in[r], diff_lambda[r], diff_subln[r], tabs, lambda_init, tm=tm)
            w_out = diff_w_out[r]
        x = _mixer_out_ffn_call(a, x, mod, w_out.astype(BF16), ln_g[i, 1], ln_b[i, 1],
                                w_in_bf, w_out_bf, (i, 1), ln_g[i, 2], ln_b[i, 2], alpha=alpha)
    return x
```

```python
import functools
import math

import jax
import jax.numpy as jnp
from jax import lax
from jax.experimental import pallas as pl
from jax.experimental.pallas import tpu as pltpu

F32 = jnp.float32
BF16 = jnp.bfloat16

HEAD_DIM = 64
ROT_DIM = HEAD_DIM // 4
ROPE_THETA = 500000.0
LN_EPS = 1e-5
RMS_EPS = 1e-6
N_ADA = 9
A_KV_RANK = 128
A_IDX_HEADS = 8
A_IDX_DIM = 32
A_TOPK_MAX = 256
C_WINDOW = 128
LANES = 128
NEG = -1e30
INT_MIN = -(2 ** 31)
LOG2E = math.log2(math.e)
Q_SCALE = HEAD_DIM ** -0.5 * LOG2E
VMEM_LIMIT = 56 * 1024 * 1024


def _cparams(n_axes):
    return pltpu.CompilerParams(dimension_semantics=("arbitrary",) * n_axes,
                                vmem_limit_bytes=VMEM_LIMIT)


def _resident(shape):
    nd = len(shape)
    return pl.BlockSpec(shape, lambda *_: (0,) * nd, pipeline_mode=pl.Buffered(1))


def _resident_layer(stacked, lead):
    rows, cols = stacked.shape[-2:]
    return pl.BlockSpec((None,) * len(lead) + (rows, cols), lambda *_: tuple(lead) + (0, 0),
                        pipeline_mode=pl.Buffered(1))


def _dot(a, b):
    return jnp.dot(a, b, preferred_element_type=F32)


def _dot_nt(a, b):
    return lax.dot_general(a, b, (((1,), (1,)), ((), ())), preferred_element_type=F32)


def _sigmoid(x):
    return 1.0 / (1.0 + jnp.exp(-x))


def _layer_norm(z, g, b):
    mu = jnp.mean(z, axis=-1, keepdims=True)
    zc = z - mu
    var = jnp.mean(zc * zc, axis=-1, keepdims=True)
    return zc * lax.rsqrt(var + LN_EPS) * g + b


def _modulate(x, mod_ref, j):
    shift = mod_ref[0, 3 * j:3 * j + 1, :]
    scale = mod_ref[0, 3 * j + 1:3 * j + 2, :]
    return (x * (1.0 + scale) + shift).astype(BF16)


def _rope(xs, c, s_up, s_dn, half):
    return (xs * c + pltpu.roll(xs, LANES - half, 1) * s_up + pltpu.roll(xs, half, 1) * s_dn)


def _split_bf16(x):
    hi = x.astype(BF16)
    return hi, (x - hi.astype(F32)).astype(BF16)


def _ada_kernel(c_ref, w_ref, b_ref, o_ref):
    c = c_ref[...]
    c_hi, c_lo = _split_bf16(c * _sigmoid(c))
    w_hi, w_lo = _split_bf16(w_ref[0])
    o_ref[0] = _dot(c_hi, w_hi) + (_dot(c_hi, w_lo) + _dot(c_lo, w_hi)) + b_ref[0]


def _ada_call(c, w_ada, b_ada):
    depth, d, nd = w_ada.shape
    b = c.shape[0]
    n_blk = nd // d
    return pl.pallas_call(
        _ada_kernel,
        out_shape=jax.ShapeDtypeStruct((depth, b, nd), F32),
        grid=(depth, n_blk),
        in_specs=[pl.BlockSpec((b, d), lambda l, n: (0, 0)),
                  pl.BlockSpec((1, d, d), lambda l, n: (l, 0, n)),
                  pl.BlockSpec((1, 1, d), lambda l, n: (l, 0, n))],
        out_specs=pl.BlockSpec((1, b, d), lambda l, n: (l, 0, n)),
        compiler_params=_cparams(2),
        name="ada_mod",
    )(c, w_ada, b_ada.reshape(depth, 1, nd))


FFN_ROWS = 1024
FFN_SUB = 512
FFN_CHUNK = 256


def _swiglu_rows(h, win_ref, wout_ref, dff):
    y = jnp.zeros((h.shape[0], wout_ref.shape[1]), F32)
    a_prev = None
    for c0 in range(0, dff, FFN_CHUNK):
        g = _dot(h, win_ref[:, c0:c0 + FFN_CHUNK])
        u = _dot(h, win_ref[:, dff + c0:dff + c0 + FFN_CHUNK])
        if a_prev is not None:
            y = y + _dot(a_prev, wout_ref[c0 - FFN_CHUNK:c0, :])
        a_prev = (g * _sigmoid(g) * u).astype(BF16)
    return y + _dot(a_prev, wout_ref[dff - FFN_CHUNK:dff, :])


def _ffn_kernel(x_ref, mod_ref, win_ref, wout_ref, g_ref, b_ref, o_ref, *, j, dff, alpha):
    gate = mod_ref[0, 3 * j + 2:3 * j + 3, :]
    for r0 in range(0, x_ref.shape[1], FFN_SUB):
        x = x_ref[0, r0:r0 + FFN_SUB, :]
        y = _swiglu_rows(_modulate(x, mod_ref, j), win_ref, wout_ref, dff)
        z = alpha * x + 0.5 * (1.0 + gate) * y
        o_ref[0, r0:r0 + FFN_SUB, :] = _layer_norm(z, g_ref[...], b_ref[...])


def _mixer_out_ffn_kernel(a_ref, x_ref, mod_ref, wo_ref, g1_ref, b1_ref, win_ref, wout_ref, g2_ref, b2_ref,
                          o_ref, *, dff, alpha):
    subs = range(0, x_ref.shape[1], FFN_SUB)
    ys = [_dot(a_ref[0, r0:r0 + FFN_SUB, :], wo_ref[...]) for r0 in subs]
    gate1 = mod_ref[0, 5:6, :]
    gate2 = mod_ref[0, 8:9, :]
    for r0, y1 in zip(subs, ys):
        x = _layer_norm(alpha * x_ref[0, r0:r0 + FFN_SUB, :] + (1.0 + gate1) * y1, g1_ref[...], b1_ref[...])
        y = _swiglu_rows(_modulate(x, mod_ref, 2), win_ref, wout_ref, dff)
        z = alpha * x + 0.5 * (1.0 + gate2) * y
        o_ref[0, r0:r0 + FFN_SUB, :] = _layer_norm(z, g2_ref[...], b2_ref[...])


def _ffn_rows(s):
    return min(FFN_ROWS, s)


def _ffn_call(x, mod, w_in, w_out, lead, g, b, *, j, alpha):
    bsz, s, d = x.shape
    dff = w_out.shape[-2]
    tm = _ffn_rows(s)
    return pl.pallas_call(
        functools.partial(_ffn_kernel, j=j, dff=dff, alpha=alpha),
        out_shape=jax.ShapeDtypeStruct(x.shape, F32),
        grid=(bsz, s // tm),
        in_specs=[pl.BlockSpec((1, tm, d), lambda bi, si: (bi, si, 0)),
                  pl.BlockSpec((1, N_ADA, d), lambda bi, si: (bi, 0, 0)),
                  _resident_layer(w_in, lead), _resident_layer(w_out, lead),
                  _resident((1, d)), _resident((1, d))],
        out_specs=pl.BlockSpec((1, tm, d), lambda bi, si: (bi, si, 0)),
        compiler_params=_cparams(2),
        name="ffn_sublayer",
    )(x, mod, w_in, w_out, g.reshape(1, d), b.reshape(1, d))


def _mixer_out_ffn_call(a, x, mod, w_o, g1, b1, w_in, w_out, lead, g2, b2, *, alpha):
    bsz, s, d = x.shape
    dff = w_out.shape[-2]
    tm = _ffn_rows(s)
    row_block = lambda width: pl.BlockSpec((1, tm, width), lambda bi, si: (bi, si, 0))
    return pl.pallas_call(
        functools.partial(_mixer_out_ffn_kernel, dff=dff, alpha=alpha),
        out_shape=jax.ShapeDtypeStruct(x.shape, F32),
        grid=(bsz, s // tm),
        in_specs=[row_block(a.shape[-1]), row_block(d),
                  pl.BlockSpec((1, N_ADA, d), lambda bi, si: (bi, 0, 0)),
                  _resident(w_o.shape), _resident((1, d)), _resident((1, d)),
                  _resident_layer(w_in, lead), _resident_layer(w_out, lead),
                  _resident((1, d)), _resident((1, d))],
        out_specs=row_block(d),
        compiler_params=_cparams(2),
        name="mixer_out_ffn",
    )(a, x, mod, w_o, g1.reshape(1, d), b1.reshape(1, d), w_in, w_out, g2.reshape(1, d), b2.reshape(1, d))


def _rope_store(dst_ref, p, col0, n_blk, tabs, half, scale=None):
    c, s_up, s_dn = tabs
    for i in range(n_blk):
        xs = _rope(p[:, col0 + i * LANES:col0 + (i + 1) * LANES], c, s_up, s_dn, half)
        if scale is not None:
            xs = xs * scale
        dst_ref[0, :, i * LANES:(i + 1) * LANES] = xs.astype(dst_ref.dtype)


def _plain_store(dst_ref, p, col0, n_blk):
    dst_ref[0] = p[:, col0:col0 + n_blk * LANES].astype(dst_ref.dtype)


def _dsa_proj_kernel(x_ref, mod_ref, w_ref, kvn_ref, wkv_ref, c_ref, su_ref, sd_ref,
                     ci_ref, sui_ref, sdi_ref,
                     q_ref, k_ref, v_ref, qi_ref, ki_ref, wi_ref, *, d):
    h = _modulate(x_ref[0], mod_ref, 1)
    p = _dot(h, w_ref[...])
    tabs = (c_ref[...], su_ref[...], sd_ref[...])
    tabs_i = (ci_ref[...], sui_ref[...], sdi_ref[...])
    nq = d // LANES
    _rope_store(q_ref, p, 0, nq, tabs, ROT_DIM // 2, scale=Q_SCALE)
    ckv = p[:, d:d + A_KV_RANK]
    ckv = ckv * lax.rsqrt(jnp.mean(ckv * ckv, axis=-1, keepdims=True) + RMS_EPS) * kvn_ref[...]
    kv = _dot(ckv.astype(BF16), wkv_ref[...])
    _rope_store(k_ref, kv, 0, 1, tabs, ROT_DIM // 2)
    _plain_store(v_ref, kv, LANES, 1)
    c0 = d + A_KV_RANK
    _rope_store(qi_ref, p, c0, 2, tabs_i, A_IDX_DIM // 8)
    _rope_store(ki_ref, p, c0 + 2 * LANES, 2, tabs_i, A_IDX_DIM // 8)
    wi_ref[0] = p[:, c0 + 4 * LANES:c0 + 5 * LANES] * (A_IDX_HEADS ** -0.5 * A_IDX_DIM ** -0.5)


def _fox_proj_kernel(x_ref, mod_ref, w_ref, fb_ref, q_ref, k_ref, v_ref, cum_ref, carry_ref, *, d, tm):
    h = _modulate(x_ref[0], mod_ref, 1)
    p = _dot(h, w_ref[...])
    q_ref[0] = (p[:, 0:d] * Q_SCALE).astype(BF16)
    k_ref[0] = p[:, d:2 * d].astype(BF16)
    v_ref[0] = p[:, 2 * d:3 * d].astype(BF16)
    f = p[:, 3 * d:3 * d + LANES] + fb_ref[...]
    logf = jnp.minimum(f, 0.0) - jnp.log1p(jnp.exp(-jnp.abs(f)))

    @pl.when(pl.program_id(1) == 0)
    def _():
        carry_ref[...] = jnp.zeros_like(carry_ref)

    hi = logf.astype(BF16)
    r1 = logf - hi.astype(F32)
    mid = r1.astype(BF16)
    lo = (r1 - mid.astype(F32)).astype(BF16)
    row = lax.broadcasted_iota(jnp.int32, (tm, tm), 0)
    col = lax.broadcasted_iota(jnp.int32, (tm, tm), 1)
    tri = (row >= col).astype(BF16)
    cum = _dot(tri, hi) + _dot(tri, mid) + _dot(tri, lo) + carry_ref[...]
    cum_ref[0] = cum * LOG2E
    carry_ref[...] = cum[tm - 1:tm, :]


def _swa_proj_kernel(x_ref, mod_ref, w_ref, c_ref, su_ref, sd_ref, q_ref, k_ref, v_ref, *, d):
    h = _modulate(x_ref[0], mod_ref, 1)
    p = _dot(h, w_ref[...])
    tabs = (c_ref[...], su_ref[...], sd_ref[...])
    nq = d // LANES
    _rope_store(q_ref, p, 0, nq, tabs, ROT_DIM // 2, scale=Q_SCALE)
    _rope_store(k_ref, p, d, 2, tabs, ROT_DIM // 2)
    _plain_store(v_ref, p, d + 2 * LANES, 2)


def _diff_proj_kernel(x_ref, mod_ref, w_ref, c_ref, su_ref, sd_ref, q_ref, k_ref, v_ref, *, d):
    h = _modulate(x_ref[0], mod_ref, 1)
    p = _dot(h, w_ref[...])
    tabs = (c_ref[...], su_ref[...], sd_ref[...])
    nq = d // LANES
    _rope_store(q_ref, p, 0, nq, tabs, ROT_DIM // 2, scale=Q_SCALE)
    _rope_store(k_ref, p, d, nq, tabs, ROT_DIM // 2)
    _plain_store(v_ref, p, 2 * d, nq)


def _proj_call(kern, x, mod, consts, tabs, outs, *, tm, scratch=(), name):
    bsz, s, d = x.shape
    in_specs = [pl.BlockSpec((1, tm, d), lambda bi, si: (bi, si, 0)),
                pl.BlockSpec((1, N_ADA, d), lambda bi, si: (bi, 0, 0))]
    in_specs += [_resident(a.shape) for a in consts]
    in_specs += [pl.BlockSpec((tm, LANES), lambda bi, si: (si, 0)) for _ in tabs]
    return pl.pallas_call(
        kern,
        out_shape=[jax.ShapeDtypeStruct((bsz, s, w), dt) for w, dt in outs],
        grid=(bsz, s // tm),
        in_specs=in_specs,
        out_specs=[pl.BlockSpec((1, tm, w), lambda bi, si: (bi, si, 0)) for w, _ in outs],
        scratch_shapes=list(scratch),
        compiler_params=_cparams(2),
        name=name,
    )(x, mod, *consts, *tabs)


def _lane_is_low(shape):
    return lax.broadcasted_iota(jnp.int32, shape, 1) < HEAD_DIM


def _split_pair(q):
    low = _lane_is_low(q.shape)
    zero = jnp.zeros_like(q)
    return jnp.where(low, q, zero), jnp.where(low, zero, q)


def _lane_tile(x, width):
    return jnp.concatenate([x] * (width // LANES), axis=1)


def _softmax_tile(s, m, row_term=None):
    tile_max = jnp.broadcast_to(jnp.max(s, axis=-1, keepdims=True), m.shape)
    if row_term is not None:
        tile_max = tile_max + row_term
    m_new = jnp.maximum(m, tile_max)
    shift = m_new if row_term is None else m_new - row_term
    p = jnp.exp2(s - _lane_tile(shift, s.shape[1])).astype(BF16)
    return p, jnp.exp2(m - m_new), m_new


def _causal_mask(n):
    row = lax.broadcasted_iota(jnp.int32, (n, n), 0)
    col = lax.broadcasted_iota(jnp.int32, (n, n), 1)
    return row >= col


def _flash_block(first, count, mask_last, logits, consume, st):
    s_prev = logits(first, mask_last and count == 1)
    for k in range(1, count):
        s_next = logits(first + k, mask_last and k == count - 1)
        st = consume(first + k - 1, s_prev, st)
        s_prev = s_next
    return consume(first + count - 1, s_prev, st)


def _flash_tiles(n_tiles, max_tiles, logits, consume, state):
    branches = [functools.partial(_flash_block, 0, c, False, logits, consume) for c in range(1, max_tiles + 1)]
    return lax.switch(n_tiles - 1, branches, state)


def _pair_values(vj):
    low = _lane_is_low(vj.shape)
    one = jnp.ones_like(vj)
    return jnp.where(low, vj, one), jnp.where(low, one, vj)


def _pair_init(t):
    m = jnp.full((t, LANES), NEG, F32)
    zacc = jnp.zeros((t, LANES), F32)
    return m, m, zacc, zacc


def _pair_consume(s, state, v0, v1, row_terms=(None, None)):
    m0, m1, acc0, acc1 = state
    p0, a0, m0 = _softmax_tile(s[0], m0, row_terms[0])
    p1, a1, m1 = _softmax_tile(s[1], m1, row_terms[1])
    return m0, m1, a0 * acc0 + _dot(p0, v0), a1 * acc1 + _dot(p1, v1)


def _pair_finish(acc0, acc1):
    low = _lane_is_low(acc0.shape)
    return jnp.where(low, acc0 / pltpu.roll(acc0, HEAD_DIM, 1), acc1 / pltpu.roll(acc1, HEAD_DIM, 1))


def _key_tile(ref, j, t):
    if isinstance(j, int):
        return ref[0, j * t:(j + 1) * t, :]
    return ref[0, pl.ds(pl.multiple_of(j * t, t), t), :]


def _fox_attn_kernel(q_ref, k_ref, v_ref, ck_ref, o_ref, *, t):
    for i in range(k_ref.shape[1] // t):
        q0, q1 = _split_pair(_key_tile(q_ref, i, t))
        cq = ck_ref[0, 0, i]
        cq0 = jnp.transpose(jnp.broadcast_to(cq[0:1, :], (LANES, t)))
        cq1 = jnp.transpose(jnp.broadcast_to(cq[1:2, :], (LANES, t)))

        def logits(j, masked):
            kj = _key_tile(k_ref, j, t)
            ck = ck_ref[0, 0, j]
            s0 = _dot_nt(q0, kj) - ck[0:1, :]
            s1 = _dot_nt(q1, kj) - ck[1:2, :]
            if masked:
                ok = _causal_mask(t)
                s0 = jnp.where(ok, s0, NEG)
                s1 = jnp.where(ok, s1, NEG)
            return s0, s1

        def consume(j, s, state):
            v0, v1 = _pair_values(_key_tile(v_ref, j, t))
            return _pair_consume(s, state, v0, v1, (cq0, cq1))

        _, _, acc0, acc1 = _flash_block(0, i + 1, True, logits, consume, _pair_init(t))
        o_ref[0, i * t:(i + 1) * t, :] = _pair_finish(acc0, acc1).astype(o_ref.dtype)


def _diff_attn_kernel(q_ref, k_ref, v_ref, lam_ref, g_ref, o_ref, *, t, lambda_init):
    lam = lam_ref[...]
    lam_val = (jnp.exp(jnp.sum(lam[0:1] * lam[1:2], axis=-1, keepdims=True))
               - jnp.exp(jnp.sum(lam[2:3] * lam[3:4], axis=-1, keepdims=True)) + lambda_init)
    m_init = jnp.full((t, LANES), NEG, F32)
    zacc = jnp.zeros((t, 2 * LANES), F32)
    for i in range(k_ref.shape[1] // t):
        q0, q1 = _split_pair(_key_tile(q_ref, i, t))

        def logits(j, masked):
            kj = _key_tile(k_ref, j, t)
            s0 = _dot_nt(q0, kj)
            s1 = _dot_nt(q1, kj)
            if masked:
                ok = _causal_mask(t)
                s0 = jnp.where(ok, s0, NEG)
                s1 = jnp.where(ok, s1, NEG)
            return s0, s1

        def consume(j, s, state):
            m0, m1, acc0, acc1 = state
            vj = _key_tile(v_ref, j, t)
            v_ext = jnp.concatenate([vj, jnp.ones_like(vj)], axis=1)
            p0, a0, m0 = _softmax_tile(s[0], m0)
            p1, a1, m1 = _softmax_tile(s[1], m1)
            acc0 = _lane_tile(a0, 2 * LANES) * acc0 + _dot(p0, v_ext)
            acc1 = _lane_tile(a1, 2 * LANES) * acc1 + _dot(p1, v_ext)
            return m0, m1, acc0, acc1

        _, _, acc0, acc1 = _flash_block(0, i + 1, True, logits, consume, (m_init, m_init, zacc, zacc))
        out = acc0[:, :LANES] / acc0[:, LANES:] - lam_val * (acc1[:, :LANES] / acc1[:, LANES:])
        out = out * lax.rsqrt(jnp.mean(out * out, axis=-1, keepdims=True) + RMS_EPS) * g_ref[...]
        o_ref[0, i * t:(i + 1) * t, :] = (out * (1.0 - lambda_init)).astype(o_ref.dtype)


def _swa_attn_kernel(sink_ref, q_ref, k_ref, v_ref, o_ref, *, t, blk):
    hp = pl.program_id(1)
    i = pl.program_id(2)
    low = _lane_is_low((blk, LANES))
    sinks = (sink_ref[2 * hp] * LOG2E, sink_ref[2 * hp + 1] * LOG2E)
    row = lax.broadcasted_iota(jnp.int32, (blk, 2 * blk), 0)
    col = lax.broadcasted_iota(jnp.int32, (blk, 2 * blk), 1)
    pending = []
    for u in range(t // blk):
        r0 = i * t + u * blk
        k0 = pl.multiple_of(jnp.maximum(r0 - blk, 0), blk)
        kj = k_ref[0, pl.ds(k0, 2 * blk), :]
        dist = (r0 + row) - (k0 + col)
        ok = (dist >= 0) & (dist < C_WINDOW)
        logits = [jnp.where(ok, _dot_nt(qe, kj), NEG) for qe in _split_pair(q_ref[0, u * blk:(u + 1) * blk, :])]
        pending.append((u, k0, logits))
    for u, k0, logits in pending:
        vj = v_ref[0, pl.ds(k0, 2 * blk), :]
        outs = []
        for s, sink in zip(logits, sinks):
            m = jnp.maximum(jnp.max(s, axis=-1, keepdims=True), sink)
            p = jnp.exp2(s - m)
            denom = jnp.sum(p, axis=-1, keepdims=True) + jnp.exp2(sink - m)
            outs.append(_dot(p.astype(BF16), vj) / denom)
        o_ref[0, u * blk:(u + 1) * blk, :] = jnp.where(low, outs[0], outs[1]).astype(o_ref.dtype)


def _sortable(score):
    bits = lax.bitcast_convert_type(score, jnp.int32)
    return bits ^ (lax.shift_right_arithmetic(bits, 31) & jnp.int32(0x7FFFFFFF))


def _sublane_fold(x):
    return jnp.sum(x.reshape(x.shape[0] // 8, 8, x.shape[1]), axis=0)


def _dsa_attn_kernel(q_ref, k_ref, v_ref, qi_ref, ki_ref, wi_ref, o_ref, keys_ref, selb_ref, *, t, k_sel):
    i = pl.program_id(1)
    hp = pl.program_id(2)

    def count_keys(pred_fn):
        def body(j, acc):
            return acc + _sublane_fold(jnp.where(pred_fn(keys_ref[j]), 1.0, 0.0))
        acc = lax.fori_loop(0, i + 1, body, jnp.zeros((8, t), F32))
        return jnp.sum(acc, axis=0, keepdims=True)

    @pl.when(hp == 0)
    def _select():
        qi = qi_ref[0]
        lane = lax.broadcasted_iota(jnp.int32, qi.shape, 1)
        qis = [jnp.where((lane >= h * A_IDX_DIM) & (lane < (h + 1) * A_IDX_DIM), qi, jnp.zeros_like(qi))
               for h in range(A_IDX_HEADS)]
        w_t = jnp.transpose(wi_ref[0])
        key_pos = lax.broadcasted_iota(jnp.int32, (t, t), 0)
        qry_pos = lax.broadcasted_iota(jnp.int32, (t, t), 1)

        def score_tile(j, masked):
            kij = ki_ref[0, pl.ds(pl.multiple_of(j * t, t), t), :]
            score = jnp.zeros((t, t), F32)
            for h in range(A_IDX_HEADS):
                score = score + w_t[h:h + 1, :] * jnp.maximum(_dot_nt(kij, qis[h]), 0.0)
            key = _sortable(score)
            if masked:
                key = jnp.where(key_pos <= qry_pos, key, INT_MIN)
            keys_ref[j] = key

        def score_body(j, c):
            score_tile(j, False)
            return c
        lax.fori_loop(0, i, score_body, 0)
        score_tile(i, True)

        def bit_body(b, thr_u):
            cand = thr_u | lax.shift_left(jnp.int32(1), 31 - b)
            cand_s = cand ^ INT_MIN
            cnt = count_keys(lambda key: key >= cand_s)
            return jnp.where(cnt >= k_sel, cand, thr_u)
        thr_u = lax.fori_loop(0, 32, bit_body, jnp.zeros((1, t), jnp.int32))
        thr = thr_u ^ INT_MIN
        need = k_sel - count_keys(lambda key: key > thr)

        earlier = (qry_pos < key_pos).astype(BF16)

        def sel_tile(j, seen, masked):
            key = keys_ref[j]
            eq = key == thr
            eq_f = jnp.where(eq, 1.0, 0.0)
            rank = _dot(earlier, eq_f.astype(BF16)) + seen
            sel = (key > thr) | (eq & (rank < need))
            if masked:
                sel = sel & (key_pos <= qry_pos)
            selb_ref[j] = jnp.transpose(jnp.where(sel, 0.0, NEG))
            return seen + jnp.sum(_sublane_fold(eq_f), axis=0, keepdims=True)

        seen = lax.fori_loop(0, i, lambda j, c: sel_tile(j, c, False), jnp.zeros((1, t), F32))
        sel_tile(i, seen, True)

    q0, q1 = _split_pair(q_ref[0])

    def logits(j, _):
        kj = _key_tile(k_ref, j, t)
        bias = selb_ref[j]
        return _dot_nt(q0, kj) + bias, _dot_nt(q1, kj) + bias

    def consume(j, s, state):
        v0, v1 = _pair_values(_key_tile(v_ref, j, t))
        return _pair_consume(s, state, v0, v1)

    _, _, acc0, acc1 = _flash_tiles(i + 1, k_ref.shape[1] // t, logits, consume, _pair_init(t))
    o_ref[0] = _pair_finish(acc0, acc1).astype(o_ref.dtype)


def _rope_lane_tables(seq, rot_dim, period):
    half = rot_dim // 2
    inv = ROPE_THETA ** (-jnp.arange(0, rot_dim, 2, dtype=F32) / rot_dim)
    ang = jnp.arange(seq, dtype=F32)[:, None] * inv[None, :]
    cos, sin = jnp.cos(ang), jnp.sin(ang)
    lane = jnp.arange(LANES) % period
    idx = jnp.where(lane < half, lane, jnp.clip(lane - half, 0, half - 1))
    first = (lane < half)[None, :]
    second = ((lane >= half) & (lane < 2 * half))[None, :]
    c = jnp.where(first | second, cos[:, idx], 1.0)
    s_up = jnp.where(first, -sin[:, idx], 0.0)
    s_dn = jnp.where(second, sin[:, idx], 0.0)
    return c.astype(F32), s_up.astype(F32), s_dn.astype(F32)


def _pad_cols(w, n):
    return jnp.pad(w, ((0, 0), (0, n - w.shape[1])))


def _attn_tile(s):
    return min(512, s)


def _mixer_dsa(x, mod, w_in, kv_norm, w_kv_up, tabs, tabs_i, *, tm):
    bsz, s, d = x.shape
    t = _attn_tile(s)
    k_sel = min(A_TOPK_MAX, s // 4)
    n_qi = A_IDX_HEADS * A_IDX_DIM
    c0 = d + A_KV_RANK
    w_ki = w_in[:, c0 + n_qi:c0 + n_qi + A_IDX_DIM]
    w_wi = w_in[:, c0 + n_qi + A_IDX_DIM:]
    w_cat = jnp.concatenate([w_in[:, :c0 + n_qi], jnp.tile(w_ki, (1, n_qi // A_IDX_DIM)),
                             _pad_cols(w_wi, LANES)], axis=1).astype(BF16)
    wk, wv = w_kv_up[:, :HEAD_DIM], w_kv_up[:, HEAD_DIM:]
    wkv = jnp.concatenate([wk, wk, wv, wv], axis=1).astype(BF16)
    q, k2, v2, qi, ki, wi = _proj_call(
        functools.partial(_dsa_proj_kernel, d=d), x, mod,
        [w_cat, kv_norm.reshape(1, -1), wkv], list(tabs) + list(tabs_i),
        [(d, BF16), (LANES, BF16), (LANES, BF16), (n_qi, BF16), (n_qi, BF16), (LANES, F32)],
        tm=tm, name="dsa_in_proj")
    nt = s // t
    return pl.pallas_call(
        functools.partial(_dsa_attn_kernel, t=t, k_sel=k_sel),
        out_shape=jax.ShapeDtypeStruct((bsz, s, d), BF16),
        grid=(bsz, nt, d // LANES),
        in_specs=[pl.BlockSpec((1, t, LANES), lambda b, i, h: (b, i, h)),
                  pl.BlockSpec((1, s, LANES), lambda b, i, h: (b, 0, 0)),
                  pl.BlockSpec((1, s, LANES), lambda b, i, h: (b, 0, 0)),
                  pl.BlockSpec((1, t, n_qi), lambda b, i, h: (b, i, 0)),
                  pl.BlockSpec((1, s, n_qi), lambda b, i, h: (b, 0, 0)),
                  pl.BlockSpec((1, t, LANES), lambda b, i, h: (b, i, 0))],
        out_specs=pl.BlockSpec((1, t, LANES), lambda b, i, h: (b, i, h)),
        scratch_shapes=[pltpu.VMEM((nt, t, t), jnp.int32), pltpu.VMEM((nt, t, t), F32)],
        compiler_params=_cparams(3),
        name="dsa_attention",
    )(q, k2, v2, qi, ki, wi)


def _mixer_fox(x, mod, w_in, f_bias, *, tm):
    bsz, s, d = x.shape
    t = _attn_tile(s)
    nh = d // HEAD_DIM
    w_cat = _pad_cols(w_in, 3 * d + LANES).astype(BF16)
    fb = _pad_cols(f_bias.reshape(1, -1), LANES)
    q, k, v, cum = _proj_call(
        functools.partial(_fox_proj_kernel, d=d, tm=tm), x, mod, [w_cat, fb], [],
        [(d, BF16), (d, BF16), (d, BF16), (LANES, F32)],
        tm=tm, scratch=[pltpu.VMEM((1, LANES), F32)], name="fox_in_proj")
    nt = s // t
    ck = jnp.transpose(cum[:, :, :nh].reshape(bsz, nt, t, nh // 2, 2), (0, 3, 1, 4, 2))
    return pl.pallas_call(
        functools.partial(_fox_attn_kernel, t=t),
        out_shape=jax.ShapeDtypeStruct((bsz, s, d), BF16),
        grid=(bsz, d // LANES),
        in_specs=[pl.BlockSpec((1, s, LANES), lambda b, h: (b, 0, h)),
                  pl.BlockSpec((1, s, LANES), lambda b, h: (b, 0, h)),
                  pl.BlockSpec((1, s, LANES), lambda b, h: (b, 0, h)),
                  pl.BlockSpec((1, 1, nt, 2, t), lambda b, h: (b, h, 0, 0, 0))],
        out_specs=pl.BlockSpec((1, s, LANES), lambda b, h: (b, 0, h)),
        compiler_params=_cparams(2),
        name="fox_attention",
    )(q, k, v, ck)


def _mixer_swa(x, mod, w_in, sinks, tabs, *, tm):
    bsz, s, d = x.shape
    t = _attn_tile(s)
    n_kv = (w_in.shape[1] - d) // (2 * HEAD_DIM)
    pairs_per_kv = (d // LANES) // n_kv
    wq, wk, wv = w_in[:, :d], w_in[:, d:d + n_kv * HEAD_DIM], w_in[:, d + n_kv * HEAD_DIM:]

    def dup(w):
        w = w.reshape(w.shape[0], n_kv, 1, HEAD_DIM)
        return jnp.broadcast_to(w, (w.shape[0], n_kv, 2, HEAD_DIM)).reshape(w.shape[0], n_kv * LANES)

    w_cat = jnp.concatenate([wq, dup(wk), dup(wv)], axis=1).astype(BF16)
    q, k2, v2 = _proj_call(
        functools.partial(_swa_proj_kernel, d=d), x, mod, [w_cat], list(tabs),
        [(d, BF16), (n_kv * LANES, BF16), (n_kv * LANES, BF16)], tm=tm, name="swa_in_proj")
    return pl.pallas_call(
        functools.partial(_swa_attn_kernel, t=t, blk=C_WINDOW),
        out_shape=jax.ShapeDtypeStruct((bsz, s, d), BF16),
        grid=(bsz, d // LANES, s // t),
        in_specs=[pl.BlockSpec(memory_space=pltpu.SMEM),
                  pl.BlockSpec((1, t, LANES), lambda b, h, i: (b, i, h)),
                  pl.BlockSpec((1, s, LANES), lambda b, h, i: (b, 0, h // pairs_per_kv)),
                  pl.BlockSpec((1, s, LANES), lambda b, h, i: (b, 0, h // pairs_per_kv))],
        out_specs=pl.BlockSpec((1, t, LANES), lambda b, h, i: (b, i, h)),
        compiler_params=_cparams(3),
        name="swa_attention",
    )(sinks.astype(F32), q, k2, v2)


def _mixer_diff(x, mod, w_in, lam, subln_g, tabs, lambda_init, *, tm):
    bsz, s, d = x.shape
    t = _attn_tile(s)
    q, k, v = _proj_call(
        functools.partial(_diff_proj_kernel, d=d), x, mod, [w_in.astype(BF16)], list(tabs),
        [(d, BF16), (d, BF16), (d, BF16)], tm=tm, name="diff_in_proj")
    return pl.pallas_call(
        functools.partial(_diff_attn_kernel, t=t, lambda_init=lambda_init),
        out_shape=jax.ShapeDtypeStruct((bsz, s, d), BF16),
        grid=(bsz, d // LANES),
        in_specs=[pl.BlockSpec((1, s, LANES), lambda b, h: (b, 0, h)),
                  pl.BlockSpec((1, s, LANES), lambda b, h: (b, 0, h)),
                  pl.BlockSpec((1, s, LANES), lambda b, h: (b, 0, h)),
                  pl.BlockSpec(lam.shape, lambda b, h: (0, 0)),
                  pl.BlockSpec((1, LANES), lambda b, h: (0, 0))],
        out_specs=pl.BlockSpec((1, s, LANES), lambda b, h: (b, 0, h)),
        compiler_params=_cparams(2),
        name="diff_attention",
    )(q, k, v, lam.astype(F32), subln_g.reshape(1, -1))


def kernel(x, c, ln_g, ln_b, w_ada, b_ada, w_ffn_in, w_ffn_out, dsa_w_in, dsa_kv_norm, dsa_w_kv_up, dsa_w_out, fox_w_in, fox_f_bias, fox_w_out, swa_w_in, swa_sinks, swa_w_out, diff_w_in, diff_lambda, diff_subln, diff_w_out):
    bsz, s, d = x.shape
    depth = w_ada.shape[0]
    n_mixers = 4
    alpha = (2 * depth) ** 0.25
    tm = min(512, s)
    tabs = _rope_lane_tables(s, ROT_DIM, HEAD_DIM)
    tabs_i = _rope_lane_tables(s, A_IDX_DIM // 4, A_IDX_DIM)
    mods = _ada_call(c, w_ada, b_ada).reshape(depth, bsz, N_ADA, d)
    w_in_bf, w_out_bf = w_ffn_in.astype(BF16), w_ffn_out.astype(BF16)
    for i in range(depth):
        m, r = i % n_mixers, i // n_mixers
        mod = mods[i]
        x = _ffn_call(x, mod, w_in_bf, w_out_bf, (i, 0), ln_g[i, 0], ln_b[i, 0], j=0, alpha=alpha)
        if m == 0:
            a = _mixer_dsa(x, mod, dsa_w_in[r], dsa_kv_norm[r], dsa_w_kv_up[r], tabs, tabs_i, tm=tm)
            w_out = dsa_w_out[r]
        elif m == 1:
            a = _mixer_fox(x, mod, fox_w_in[r], fox_f_bias[r], tm=tm)
            w_out = fox_w_out[r]
        elif m == 2:
            a = _mixer_swa(x, mod, swa_w_in[r], swa_sinks[r], tabs, tm=tm)
            w_out = swa_w_out[r]
        else:
            lambda_init = 0.8 - 0.6 * math.exp(-0.3 * i)
            a = _mixer_diff(x, mod, diff_w_in[r], diff_lambda[r], diff_subln[r], tabs, lambda_init, tm=tm)
            w_out = diff_w_out[r]
        x = _mixer_out_ffn_call(a, x, mod, w_out.astype(BF16), ln_g[i, 1], ln_b[i, 1],
                                w_in_bf, w_out_bf, (i, 1), ln_g[i, 2], ln_b[i, 2], alpha=alpha)
    return x
```

```python
import functools
import math

import jax
import jax.numpy as jnp
from jax import lax
from jax.experimental import pallas as pl
from jax.experimental.pallas import tpu as pltpu

F32 = jnp.float32
BF16 = jnp.bfloat16

HEAD_DIM = 64
ROT_DIM = HEAD_DIM // 4
ROPE_THETA = 500000.0
LN_EPS = 1e-5
RMS_EPS = 1e-6
N_ADA = 9
A_KV_RANK = 128
A_IDX_HEADS = 8
A_IDX_DIM = 32
A_TOPK_MAX = 256
C_WINDOW = 128
LANES = 128
NEG = -1e30
INT_MIN = -(2 ** 31)
LOG2E = math.log2(math.e)
Q_SCALE = HEAD_DIM ** -0.5 * LOG2E
VMEM_LIMIT = 56 * 1024 * 1024


def _cparams(n_axes):
    return pltpu.CompilerParams(dimension_semantics=("arbitrary",) * n_axes,
                                vmem_limit_bytes=VMEM_LIMIT)


def _resident(shape):
    nd = len(shape)
    return pl.BlockSpec(shape, lambda *_: (0,) * nd, pipeline_mode=pl.Buffered(1))


def _resident_layer(stacked, lead):
    rows, cols = stacked.shape[-2:]
    return pl.BlockSpec((None,) * len(lead) + (rows, cols), lambda *_: tuple(lead) + (0, 0),
                        pipeline_mode=pl.Buffered(1))


def _dot(a, b):
    return jnp.dot(a, b, preferred_element_type=F32)


def _dot_nt(a, b):
    return lax.dot_general(a, b, (((1,), (1,)), ((), ())), preferred_element_type=F32)


def _sigmoid(x):
    return 1.0 / (1.0 + jnp.exp(-x))


def _layer_norm(z, g, b):
    mu = jnp.mean(z, axis=-1, keepdims=True)
    zc = z - mu
    var = jnp.mean(zc * zc, axis=-1, keepdims=True)
    return zc * lax.rsqrt(var + LN_EPS) * g + b


def _modulate(x, mod_ref, j):
    shift = mod_ref[0, 3 * j:3 * j + 1, :]
    scale = mod_ref[0, 3 * j + 1:3 * j + 2, :]
    return (x * (1.0 + scale) + shift).astype(BF16)


def _rope(xs, c, s_up, s_dn, half):
    return (xs * c + pltpu.roll(xs, LANES - half, 1) * s_up + pltpu.roll(xs, half, 1) * s_dn)


def _split_bf16(x):
    hi = x.astype(BF16)
    return hi, (x - hi.astype(F32)).astype(BF16)


def _ada_kernel(c_ref, w_ref, b_ref, o_ref):
    c = c_ref[...]
    c_hi, c_lo = _split_bf16(c * _sigmoid(c))
    w_hi, w_lo = _split_bf16(w_ref[0])
    o_ref[0] = _dot(c_hi, w_hi) + (_dot(c_hi, w_lo) + _dot(c_lo, w_hi)) + b_ref[0]


def _ada_call(c, w_ada, b_ada):
    depth, d, nd = w_ada.shape
    b = c.shape[0]
    n_blk = nd // d
    return pl.pallas_call(
        _ada_kernel,
        out_shape=jax.ShapeDtypeStruct((depth, b, nd), F32),
        grid=(depth, n_blk),
        in_specs=[pl.BlockSpec((b, d), lambda l, n: (0, 0)),
                  pl.BlockSpec((1, d, d), lambda l, n: (l, 0, n)),
                  pl.BlockSpec((1, 1, d), lambda l, n: (l, 0, n))],
        out_specs=pl.BlockSpec((1, b, d), lambda l, n: (l, 0, n)),
        compiler_params=_cparams(2),
        name="ada_mod",
    )(c, w_ada, b_ada.reshape(depth, 1, nd))


FFN_ROWS = 1024
FFN_SUB = 512
FFN_CHUNK = 256


def _swiglu_rows(h, win_ref, wout_ref, dff):
    y = jnp.zeros((h.shape[0], wout_ref.shape[1]), F32)
    a_prev = None
    for c0 in range(0, dff, FFN_CHUNK):
        g = _dot(h, win_ref[:, c0:c0 + FFN_CHUNK])
        u = _dot(h, win_ref[:, dff + c0:dff + c0 + FFN_CHUNK])
        if a_prev is not None:
            y = y + _dot(a_prev, wout_ref[c0 - FFN_CHUNK:c0, :])
        a_prev = (g * _sigmoid(g) * u).astype(BF16)
    return y + _dot(a_prev, wout_ref[dff - FFN_CHUNK:dff, :])


def _ffn_kernel(x_ref, mod_ref, win_ref, wout_ref, g_ref, b_ref, o_ref, *, j, dff, alpha):
    gate = mod_ref[0, 3 * j + 2:3 * j + 3, :]
    for r0 in range(0, x_ref.shape[1], FFN_SUB):
        x = x_ref[0, r0:r0 + FFN_SUB, :]
        y = _swiglu_rows(_modulate(x, mod_ref, j), win_ref, wout_ref, dff)
        z = alpha * x + 0.5 * (1.0 + gate) * y
        o_ref[0, r0:r0 + FFN_SUB, :] = _layer_norm(z, g_ref[...], b_ref[...])


def _mixer_out_ffn_kernel(a_ref, x_ref, mod_ref, wo_ref, g1_ref, b1_ref, win_ref, wout_ref, g2_ref, b2_ref,
                          o_ref, *, dff, alpha):
    subs = range(0, x_ref.shape[1], FFN_SUB)
    ys = [_dot(a_ref[0, r0:r0 + FFN_SUB, :], wo_ref[...]) for r0 in subs]
    gate1 = mod_ref[0, 5:6, :]
    gate2 = mod_ref[0, 8:9, :]
    for r0, y1 in zip(subs, ys):
        x = _layer_norm(alpha * x_ref[0, r0:r0 + FFN_SUB, :] + (1.0 + gate1) * y1, g1_ref[...], b1_ref[...])
        y = _swiglu_rows(_modulate(x, mod_ref, 2), win_ref, wout_ref, dff)
        z = alpha * x + 0.5 * (1.0 + gate2) * y
        o_ref[0, r0:r0 + FFN_SUB, :] = _layer_norm(z, g2_ref[...], b2_ref[...])


def _ffn_rows(s):
    return min(FFN_ROWS, s)


def _ffn_call(x, mod, w_in, w_out, lead, g, b, *, j, alpha):
    bsz, s, d = x.shape
    dff = w_out.shape[-2]
    tm = _ffn_rows(s)
    return pl.pallas_call(
        functools.partial(_ffn_kernel, j=j, dff=dff, alpha=alpha),
        out_shape=jax.ShapeDtypeStruct(x.shape, F32),
        grid=(bsz, s // tm),
        in_specs=[pl.BlockSpec((1, tm, d), lambda bi, si: (bi, si, 0)),
                  pl.BlockSpec((1, N_ADA, d), lambda bi, si: (bi, 0, 0)),
                  _resident_layer(w_in, lead), _resident_layer(w_out, lead),
                  _resident((1, d)), _resident((1, d))],
        out_specs=pl.BlockSpec((1, tm, d), lambda bi, si: (bi, si, 0)),
        compiler_params=_cparams(2),
        name="ffn_sublayer",
    )(x, mod, w_in, w_out, g.reshape(1, d), b.reshape(1, d))


def _mixer_out_ffn_call(a, x, mod, w_o, g1, b1, w_in, w_out, lead, g2, b2, *, alpha):
    bsz, s, d = x.shape
    dff = w_out.shape[-2]
    tm = _ffn_rows(s)
    row_block = lambda width: pl.BlockSpec((1, tm, width), lambda bi, si: (bi, si, 0))
    return pl.pallas_call(
        functools.partial(_mixer_out_ffn_kernel, dff=dff, alpha=alpha),
        out_shape=jax.ShapeDtypeStruct(x.shape, F32),
        grid=(bsz, s // tm),
        in_specs=[row_block(a.shape[-1]), row_block(d),
                  pl.BlockSpec((1, N_ADA, d), lambda bi, si: (bi, 0, 0)),
                  _resident(w_o.shape), _resident((1, d)), _resident((1, d)),
                  _resident_layer(w_in, lead), _resident_layer(w_out, lead),
                  _resident((1, d)), _resident((1, d))],
        out_specs=row_block(d),
        compiler_params=_cparams(2),
        name="mixer_out_ffn",
    )(a, x, mod, w_o, g1.reshape(1, d), b1.reshape(1, d), w_in, w_out, g2.reshape(1, d), b2.reshape(1, d))


def _rope_store(dst_ref, p, col0, n_blk, tabs, half, scale=None):
    c, s_up, s_dn = tabs
    for i in range(n_blk):
        xs = _rope(p[:, col0 + i * LANES:col0 + (i + 1) * LANES], c, s_up, s_dn, half)
        if scale is not None:
            xs = xs * scale
        dst_ref[0, :, i * LANES:(i + 1) * LANES] = xs.astype(dst_ref.dtype)


def _plain_store(dst_ref, p, col0, n_blk):
    dst_ref[0] = p[:, col0:col0 + n_blk * LANES].astype(dst_ref.dtype)


def _dsa_proj_kernel(x_ref, mod_ref, w_ref, kvn_ref, wkv_ref, c_ref, su_ref, sd_ref,
                     ci_ref, sui_ref, sdi_ref,
                     q_ref, k_ref, v_ref, qi_ref, ki_ref, wi_ref, *, d):
    h = _modulate(x_ref[0], mod_ref, 1)
    p = _dot(h, w_ref[...])
    tabs = (c_ref[...], su_ref[...], sd_ref[...])
    tabs_i = (ci_ref[...], sui_ref[...], sdi_ref[...])
    nq = d // LANES
    _rope_store(q_ref, p, 0, nq, tabs, ROT_DIM // 2, scale=Q_SCALE)
    ckv = p[:, d:d + A_KV_RANK]
    ckv = ckv * lax.rsqrt(jnp.mean(ckv * ckv, axis=-1, keepdims=True) + RMS_EPS) * kvn_ref[...]
    kv = _dot(ckv.astype(BF16), wkv_ref[...])
    _rope_store(k_ref, kv, 0, 1, tabs, ROT_DIM // 2)
    _plain_store(v_ref, kv, LANES, 1)
    c0 = d + A_KV_RANK
    _rope_store(qi_ref, p, c0, 2, tabs_i, A_IDX_DIM // 8)
    _rope_store(ki_ref, p, c0 + 2 * LANES, 2, tabs_i, A_IDX_DIM // 8)
    wi_ref[0] = p[:, c0 + 4 * LANES:c0 + 5 * LANES] * (A_IDX_HEADS ** -0.5 * A_IDX_DIM ** -0.5)


def _fox_proj_kernel(x_ref, mod_ref, w_ref, fb_ref, q_ref, k_ref, v_ref, cum_ref, carry_ref, *, d, tm):
    h = _modulate(x_ref[0], mod_ref, 1)
    p = _dot(h, w_ref[...])
    q_ref[0] = (p[:, 0:d] * Q_SCALE).astype(BF16)
    k_ref[0] = p[:, d:2 * d].astype(BF16)
    v_ref[0] = p[:, 2 * d:3 * d].astype(BF16)
    f = p[:, 3 * d:3 * d + LANES] + fb_ref[...]
    logf = jnp.minimum(f, 0.0) - jnp.log1p(jnp.exp(-jnp.abs(f)))

    @pl.when(pl.program_id(1) == 0)
    def _():
        carry_ref[...] = jnp.zeros_like(carry_ref)

    hi = logf.astype(BF16)
    r1 = logf - hi.astype(F32)
    mid = r1.astype(BF16)
    lo = (r1 - mid.astype(F32)).astype(BF16)
    row = lax.broadcasted_iota(jnp.int32, (tm, tm), 0)
    col = lax.broadcasted_iota(jnp.int32, (tm, tm), 1)
    tri = (row >= col).astype(BF16)
    cum = _dot(tri, hi) + _dot(tri, mid) + _dot(tri, lo) + carry_ref[...]
    cum_ref[0] = cum * LOG2E
    carry_ref[...] = cum[tm - 1:tm, :]


def _swa_proj_kernel(x_ref, mod_ref, w_ref, c_ref, su_ref, sd_ref, q_ref, k_ref, v_ref, *, d):
    h = _modulate(x_ref[0], mod_ref, 1)
    p = _dot(h, w_ref[...])
    tabs = (c_ref[...], su_ref[...], sd_ref[...])
    nq = d // LANES
    _rope_store(q_ref, p, 0, nq, tabs, ROT_DIM // 2, scale=Q_SCALE)
    _rope_store(k_ref, p, d, 2, tabs, ROT_DIM // 2)
    _plain_store(v_ref, p, d + 2 * LANES, 2)


def _diff_proj_kernel(x_ref, mod_ref, w_ref, c_ref, su_ref, sd_ref, q_ref, k_ref, v_ref, *, d):
    h = _modulate(x_ref[0], mod_ref, 1)
    p = _dot(h, w_ref[...])
    tabs = (c_ref[...], su_ref[...], sd_ref[...])
    nq = d // LANES
    _rope_store(q_ref, p, 0, nq, tabs, ROT_DIM // 2, scale=Q_SCALE)
    _rope_store(k_ref, p, d, nq, tabs, ROT_DIM // 2)
    _plain_store(v_ref, p, 2 * d, nq)


def _proj_call(kern, x, mod, consts, tabs, outs, *, tm, scratch=(), name):
    bsz, s, d = x.shape
    in_specs = [pl.BlockSpec((1, tm, d), lambda bi, si: (bi, si, 0)),
                pl.BlockSpec((1, N_ADA, d), lambda bi, si: (bi, 0, 0))]
    in_specs += [_resident(a.shape) for a in consts]
    in_specs += [pl.BlockSpec((tm, LANES), lambda bi, si: (si, 0)) for _ in tabs]
    return pl.pallas_call(
        kern,
        out_shape=[jax.ShapeDtypeStruct((bsz, s, w), dt) for w, dt in outs],
        grid=(bsz, s // tm),
        in_specs=in_specs,
        out_specs=[pl.BlockSpec((1, tm, w), lambda bi, si: (bi, si, 0)) for w, _ in outs],
        scratch_shapes=list(scratch),
        compiler_params=_cparams(2),
        name=name,
    )(x, mod, *consts, *tabs)


def _lane_is_low(shape):
    return lax.broadcasted_iota(jnp.int32, shape, 1) < HEAD_DIM


def _split_pair(q):
    low = _lane_is_low(q.shape)
    zero = jnp.zeros_like(q)
    return jnp.where(low, q, zero), jnp.where(low, zero, q)


def _lane_tile(x, width):
    return jnp.concatenate([x] * (width // LANES), axis=1)


def _softmax_tile(s, m, row_term=None):
    tile_max = jnp.broadcast_to(jnp.max(s, axis=-1, keepdims=True), m.shape)
    if row_term is not None:
        tile_max = tile_max + row_term
    m_new = jnp.maximum(m, tile_max)
    shift = m_new if row_term is None else m_new - row_term
    p = jnp.exp2(s - _lane_tile(shift, s.shape[1])).astype(BF16)
    return p, jnp.exp2(m - m_new), m_new


def _causal_mask(n):
    row = lax.broadcasted_iota(jnp.int32, (n, n), 0)
    col = lax.broadcasted_iota(jnp.int32, (n, n), 1)
    return row >= col


def _flash_block(first, count, mask_last, logits, consume, st):
    s_prev = logits(first, mask_last and count == 1)
    for k in range(1, count):
        s_next = logits(first + k, mask_last and k == count - 1)
        st = consume(first + k - 1, s_prev, st)
        s_prev = s_next
    return consume(first + count - 1, s_prev, st)


def _pair_values(vj):
    low = _lane_is_low(vj.shape)
    one = jnp.ones_like(vj)
    return jnp.where(low, vj, one), jnp.where(low, one, vj)


def _pair_init(t):
    m = jnp.full((t, LANES), NEG, F32)
    zacc = jnp.zeros((t, LANES), F32)
    return m, m, zacc, zacc


def _pair_consume(s, state, v0, v1, row_terms=(None, None)):
    m0, m1, acc0, acc1 = state
    p0, a0, m0 = _softmax_tile(s[0], m0, row_terms[0])
    p1, a1, m1 = _softmax_tile(s[1], m1, row_terms[1])
    return m0, m1, a0 * acc0 + _dot(p0, v0), a1 * acc1 + _dot(p1, v1)


def _pair_finish(acc0, acc1):
    low = _lane_is_low(acc0.shape)
    return jnp.where(low, acc0 / pltpu.roll(acc0, HEAD_DIM, 1), acc1 / pltpu.roll(acc1, HEAD_DIM, 1))


def _key_tile(ref, j, t):
    if isinstance(j, int):
        return ref[0, j * t:(j + 1) * t, :]
    return ref[0, pl.ds(pl.multiple_of(j * t, t), t), :]


def _fox_attn_kernel(q_ref, k_ref, v_ref, ck_ref, o_ref, *, t):
    for i in range(k_ref.shape[1] // t):
        q0, q1 = _split_pair(_key_tile(q_ref, i, t))
        cq = ck_ref[0, 0, i]
        cq0 = jnp.transpose(jnp.broadcast_to(cq[0:1, :], (LANES, t)))
        cq1 = jnp.transpose(jnp.broadcast_to(cq[1:2, :], (LANES, t)))

        def logits(j, masked):
            kj = _key_tile(k_ref, j, t)
            ck = ck_ref[0, 0, j]
            s0 = _dot_nt(q0, kj) - ck[0:1, :]
            s1 = _dot_nt(q1, kj) - ck[1:2, :]
            if masked:
                ok = _causal_mask(t)
                s0 = jnp.where(ok, s0, NEG)
                s1 = jnp.where(ok, s1, NEG)
            return s0, s1

        def consume(j, s, state):
            v0, v1 = _pair_values(_key_tile(v_ref, j, t))
            return _pair_consume(s, state, v0, v1, (cq0, cq1))

        _, _, acc0, acc1 = _flash_block(0, i + 1, True, logits, consume, _pair_init(t))
        o_ref[0, i * t:(i + 1) * t, :] = _pair_finish(acc0, acc1).astype(o_ref.dtype)


def _diff_attn_kernel(q_ref, k_ref, v_ref, lam_ref, g_ref, o_ref, *, t, lambda_init):
    lam = lam_ref[...]
    lam_val = (jnp.exp(jnp.sum(lam[0:1] * lam[1:2], axis=-1, keepdims=True))
               - jnp.exp(jnp.sum(lam[2:3] * lam[3:4], axis=-1, keepdims=True)) + lambda_init)
    m_init = jnp.full((t, LANES), NEG, F32)
    zacc = jnp.zeros((t, 2 * LANES), F32)
    for i in range(k_ref.shape[1] // t):
        q0, q1 = _split_pair(_key_tile(q_ref, i, t))

        def logits(j, masked):
            kj = _key_tile(k_ref, j, t)
            s0 = _dot_nt(q0, kj)
            s1 = _dot_nt(q1, kj)
            if masked:
                ok = _causal_mask(t)
                s0 = jnp.where(ok, s0, NEG)
                s1 = jnp.where(ok, s1, NEG)
            return s0, s1

        def consume(j, s, state):
            m0, m1, acc0, acc1 = state
            vj = _key_tile(v_ref, j, t)
            v_ext = jnp.concatenate([vj, jnp.ones_like(vj)], axis=1)
            p0, a0, m0 = _softmax_tile(s[0], m0)
            p1, a1, m1 = _softmax_tile(s[1], m1)
            acc0 = _lane_tile(a0, 2 * LANES) * acc0 + _dot(p0, v_ext)
            acc1 = _lane_tile(a1, 2 * LANES) * acc1 + _dot(p1, v_ext)
            return m0, m1, acc0, acc1

        _, _, acc0, acc1 = _flash_block(0, i + 1, True, logits, consume, (m_init, m_init, zacc, zacc))
        out = acc0[:, :LANES] / acc0[:, LANES:] - lam_val * (acc1[:, :LANES] / acc1[:, LANES:])
        out = out * lax.rsqrt(jnp.mean(out * out, axis=-1, keepdims=True) + RMS_EPS) * g_ref[...]
        o_ref[0, i * t:(i + 1) * t, :] = (out * (1.0 - lambda_init)).astype(o_ref.dtype)


def _swa_attn_kernel(sink_ref, q_ref, k_ref, v_ref, o_ref, *, t, blk):
    hp = pl.program_id(1)
    low = _lane_is_low((blk, LANES))
    sinks = (sink_ref[2 * hp] * LOG2E, sink_ref[2 * hp + 1] * LOG2E)
    row = lax.broadcasted_iota(jnp.int32, (blk, 2 * blk), 0)
    col = lax.broadcasted_iota(jnp.int32, (blk, 2 * blk), 1)

    def window(offset):
        dist = row + offset - col
        return (dist >= 0) & (dist < C_WINDOW)

    ok_first, ok_rest = window(0), window(blk)
    for g0 in range(0, q_ref.shape[1], t):
        pending = []
        for r0 in range(g0, g0 + t, blk):
            k0 = max(r0 - blk, 0)
            kj = k_ref[0, k0:k0 + 2 * blk, :]
            ok = ok_first if r0 == 0 else ok_rest
            logits = [jnp.where(ok, _dot_nt(qe, kj), NEG) for qe in _split_pair(q_ref[0, r0:r0 + blk, :])]
            pending.append((r0, k0, logits))
        for r0, k0, logits in pending:
            vj = v_ref[0, k0:k0 + 2 * blk, :]
            outs = []
            for s, sink in zip(logits, sinks):
                m = jnp.maximum(jnp.max(s, axis=-1, keepdims=True), sink)
                p = jnp.exp2(s - m)
                denom = jnp.sum(p, axis=-1, keepdims=True) + jnp.exp2(sink - m)
                outs.append(_dot(p.astype(BF16), vj) / denom)
            o_ref[0, r0:r0 + blk, :] = jnp.where(low, outs[0], outs[1]).astype(o_ref.dtype)


def _sortable(score):
    bits = lax.bitcast_convert_type(score, jnp.int32)
    return bits ^ (lax.shift_right_arithmetic(bits, 31) & jnp.int32(0x7FFFFFFF))


def _sublane_fold(x):
    return jnp.sum(x.reshape(x.shape[0] // 8, 8, x.shape[1]), axis=0)


def _causal_tile(i, j):
    return i * (i + 1) // 2 + j


def _dsa_select(i, qi_ref, ki_ref, wi_ref, keys_ref, selb_ref, *, t, k_sel):
    qi = qi_ref[0, i * t:(i + 1) * t, :]
    lane = lax.broadcasted_iota(jnp.int32, qi.shape, 1)
    qis = [jnp.where((lane >= h * A_IDX_DIM) & (lane < (h + 1) * A_IDX_DIM), qi, jnp.zeros_like(qi))
           for h in range(A_IDX_HEADS)]
    w_t = jnp.transpose(wi_ref[0, i * t:(i + 1) * t, :])
    key_pos = lax.broadcasted_iota(jnp.int32, (t, t), 0)
    qry_pos = lax.broadcasted_iota(jnp.int32, (t, t), 1)

    def count_keys(pred_fn):
        def body(j, acc):
            return acc + _sublane_fold(jnp.where(pred_fn(keys_ref[j]), 1.0, 0.0))
        acc = lax.fori_loop(0, i + 1, body, jnp.zeros((8, t), F32))
        return jnp.sum(acc, axis=0, keepdims=True)

    def score_tile(j, masked):
        kij = _key_tile(ki_ref, j, t)
        score = jnp.zeros((t, t), F32)
        for h in range(A_IDX_HEADS):
            score = score + w_t[h:h + 1, :] * jnp.maximum(_dot_nt(kij, qis[h]), 0.0)
        key = _sortable(score)
        if masked:
            key = jnp.where(key_pos <= qry_pos, key, INT_MIN)
        keys_ref[j] = key

    def score_body(j, c):
        score_tile(j, False)
        return c
    lax.fori_loop(0, i, score_body, 0)
    score_tile(i, True)

    def bit_body(b, thr_u):
        cand = thr_u | lax.shift_left(jnp.int32(1), 31 - b)
        cand_s = cand ^ INT_MIN
        cnt = count_keys(lambda key: key >= cand_s)
        return jnp.where(cnt >= k_sel, cand, thr_u)
    thr_u = lax.fori_loop(0, 32, bit_body, jnp.zeros((1, t), jnp.int32))
    thr = thr_u ^ INT_MIN
    need = k_sel - count_keys(lambda key: key > thr)

    earlier = (qry_pos < key_pos).astype(BF16)

    def sel_tile(j, seen, masked):
        key = keys_ref[j]
        eq = key == thr
        eq_f = jnp.where(eq, 1.0, 0.0)
        rank = _dot(earlier, eq_f.astype(BF16)) + seen
        sel = (key > thr) | (eq & (rank < need))
        if masked:
            sel = sel & (key_pos <= qry_pos)
        selb_ref[_causal_tile(i, j)] = jnp.transpose(jnp.where(sel, 0.0, NEG)).astype(selb_ref.dtype)
        return seen + jnp.sum(_sublane_fold(eq_f), axis=0, keepdims=True)

    seen = lax.fori_loop(0, i, lambda j, c: sel_tile(j, c, False), jnp.zeros((1, t), F32))
    sel_tile(i, seen, True)


def _dsa_attn_kernel(q_ref, k_ref, v_ref, qi_ref, ki_ref, wi_ref, o_ref, keys_ref, selb_ref, *, t, k_sel):
    nt = k_ref.shape[1] // t

    @pl.when(pl.program_id(1) == 0)
    def _select_all():
        for i in range(nt):
            _dsa_select(i, qi_ref, ki_ref, wi_ref, keys_ref, selb_ref, t=t, k_sel=k_sel)

    for i in range(nt):
        q0, q1 = _split_pair(_key_tile(q_ref, i, t))

        def logits(j, _):
            kj = _key_tile(k_ref, j, t)
            bias = selb_ref[_causal_tile(i, j)].astype(F32)
            return _dot_nt(q0, kj) + bias, _dot_nt(q1, kj) + bias

        def consume(j, s, state):
            v0, v1 = _pair_values(_key_tile(v_ref, j, t))
            return _pair_consume(s, state, v0, v1)

        _, _, acc0, acc1 = _flash_block(0, i + 1, False, logits, consume, _pair_init(t))
        o_ref[0, i * t:(i + 1) * t, :] = _pair_finish(acc0, acc1).astype(o_ref.dtype)


def _rope_lane_tables(seq, rot_dim, period):
    half = rot_dim // 2
    inv = ROPE_THETA ** (-jnp.arange(0, rot_dim, 2, dtype=F32) / rot_dim)
    ang = jnp.arange(seq, dtype=F32)[:, None] * inv[None, :]
    cos, sin = jnp.cos(ang), jnp.sin(ang)
    lane = jnp.arange(LANES) % period
    idx = jnp.where(lane < half, lane, jnp.clip(lane - half, 0, half - 1))
    first = (lane < half)[None, :]
    second = ((lane >= half) & (lane < 2 * half))[None, :]
    c = jnp.where(first | second, cos[:, idx], 1.0)
    s_up = jnp.where(first, -sin[:, idx], 0.0)
    s_dn = jnp.where(second, sin[:, idx], 0.0)
    return c.astype(F32), s_up.astype(F32), s_dn.astype(F32)


def _pad_cols(w, n):
    return jnp.pad(w, ((0, 0), (0, n - w.shape[1])))


def _attn_tile(s):
    return min(512, s)


def _mixer_dsa(x, mod, w_in, kv_norm, w_kv_up, tabs, tabs_i, *, tm):
    bsz, s, d = x.shape
    t = _attn_tile(s)
    k_sel = min(A_TOPK_MAX, s // 4)
    n_qi = A_IDX_HEADS * A_IDX_DIM
    c0 = d + A_KV_RANK
    w_ki = w_in[:, c0 + n_qi:c0 + n_qi + A_IDX_DIM]
    w_wi = w_in[:, c0 + n_qi + A_IDX_DIM:]
    w_cat = jnp.concatenate([w_in[:, :c0 + n_qi], jnp.tile(w_ki, (1, n_qi // A_IDX_DIM)),
                             _pad_cols(w_wi, LANES)], axis=1).astype(BF16)
    wk, wv = w_kv_up[:, :HEAD_DIM], w_kv_up[:, HEAD_DIM:]
    wkv = jnp.concatenate([wk, wk, wv, wv], axis=1).astype(BF16)
    q, k2, v2, qi, ki, wi = _proj_call(
        functools.partial(_dsa_proj_kernel, d=d), x, mod,
        [w_cat, kv_norm.reshape(1, -1), wkv], list(tabs) + list(tabs_i),
        [(d, BF16), (LANES, BF16), (LANES, BF16), (n_qi, BF16), (n_qi, BF16), (LANES, F32)],
        tm=tm, name="dsa_in_proj")
    nt = s // t

    def per_batch(width):
        return pl.BlockSpec((1, s, width), lambda b, h: (b, 0, 0), pipeline_mode=pl.Buffered(1))

    return pl.pallas_call(
        functools.partial(_dsa_attn_kernel, t=t, k_sel=k_sel),
        out_shape=jax.ShapeDtypeStruct((bsz, s, d), BF16),
        grid=(bsz, d // LANES),
        in_specs=[pl.BlockSpec((1, s, LANES), lambda b, h: (b, 0, h)),
                  per_batch(LANES), per_batch(LANES), per_batch(n_qi), per_batch(n_qi), per_batch(LANES)],
        out_specs=pl.BlockSpec((1, s, LANES), lambda b, h: (b, 0, h)),
        scratch_shapes=[pltpu.VMEM((nt, t, t), jnp.int32),
                        pltpu.VMEM((nt * (nt + 1) // 2, t, t), BF16)],
        compiler_params=_cparams(2),
        name="dsa_attention",
    )(q, k2, v2, qi, ki, wi)


def _mixer_fox(x, mod, w_in, f_bias, *, tm):
    bsz, s, d = x.shape
    t = _attn_tile(s)
    nh = d // HEAD_DIM
    w_cat = _pad_cols(w_in, 3 * d + LANES).astype(BF16)
    fb = _pad_cols(f_bias.reshape(1, -1), LANES)
    q, k, v, cum = _proj_call(
        functools.partial(_fox_proj_kernel, d=d, tm=tm), x, mod, [w_cat, fb], [],
        [(d, BF16), (d, BF16), (d, BF16), (LANES, F32)],
        tm=tm, scratch=[pltpu.VMEM((1, LANES), F32)], name="fox_in_proj")
    nt = s // t
    ck = jnp.transpose(cum[:, :, :nh].reshape(bsz, nt, t, nh // 2, 2), (0, 3, 1, 4, 2))
    return pl.pallas_call(
        functools.partial(_fox_attn_kernel, t=t),
        out_shape=jax.ShapeDtypeStruct((bsz, s, d), BF16),
        grid=(bsz, d // LANES),
        in_specs=[pl.BlockSpec((1, s, LANES), lambda b, h: (b, 0, h)),
                  pl.BlockSpec((1, s, LANES), lambda b, h: (b, 0, h)),
                  pl.BlockSpec((1, s, LANES), lambda b, h: (b, 0, h)),
                  pl.BlockSpec((1, 1, nt, 2, t), lambda b, h: (b, h, 0, 0, 0))],
        out_specs=pl.BlockSpec((1, s, LANES), lambda b, h: (b, 0, h)),
        compiler_params=_cparams(2),
        name="fox_attention",
    )(q, k, v, ck)


def _mixer_swa(x, mod, w_in, sinks, tabs, *, tm):
    bsz, s, d = x.shape
    t = _attn_tile(s)
    n_kv = (w_in.shape[1] - d) // (2 * HEAD_DIM)
    pairs_per_kv = (d // LANES) // n_kv
    wq, wk, wv = w_in[:, :d], w_in[:, d:d + n_kv * HEAD_DIM], w_in[:, d + n_kv * HEAD_DIM:]

    def dup(w):
        w = w.reshape(w.shape[0], n_kv, 1, HEAD_DIM)
        return jnp.broadcast_to(w, (w.shape[0], n_kv, 2, HEAD_DIM)).reshape(w.shape[0], n_kv * LANES)

    w_cat = jnp.concatenate([wq, dup(wk), dup(wv)], axis=1).astype(BF16)
    q, k2, v2 = _proj_call(
        functools.partial(_swa_proj_kernel, d=d), x, mod, [w_cat], list(tabs),
        [(d, BF16), (n_kv * LANES, BF16), (n_kv * LANES, BF16)], tm=tm, name="swa_in_proj")
    return pl.pallas_call(
        functools.partial(_swa_attn_kernel, t=t, blk=C_WINDOW),
        out_shape=jax.ShapeDtypeStruct((bsz, s, d), BF16),
        grid=(bsz, d // LANES),
        in_specs=[pl.BlockSpec(memory_space=pltpu.SMEM),
                  pl.BlockSpec((1, s, LANES), lambda b, h: (b, 0, h)),
                  pl.BlockSpec((1, s, LANES), lambda b, h: (b, 0, h // pairs_per_kv)),
                  pl.BlockSpec((1, s, LANES), lambda b, h: (b, 0, h // pairs_per_kv))],
        out_specs=pl.BlockSpec((1, s, LANES), lambda b, h: (b, 0, h)),
        compiler_params=_cparams(2),
        name="swa_attention",
    )(sinks.astype(F32), q, k2, v2)


def _mixer_diff(x, mod, w_in, lam, subln_g, tabs, lambda_init, *, tm):
    bsz, s, d = x.shape
    t = _attn_tile(s)
    q, k, v = _proj_call(
        functools.partial(_diff_proj_kernel, d=d), x, mod, [w_in.astype(BF16)], list(tabs),
        [(d, BF16), (d, BF16), (d, BF16)], tm=tm, name="diff_in_proj")
    return pl.pallas_call(
        functools.partial(_diff_attn_kernel, t=t, lambda_init=lambda_init),
        out_shape=jax.ShapeDtypeStruct((bsz, s, d), BF16),
        grid=(bsz, d // LANES),
        in_specs=[pl.BlockSpec((1, s, LANES), lambda b, h: (b, 0, h)),
                  pl.BlockSpec((1, s, LANES), lambda b, h: (b, 0, h)),
                  pl.BlockSpec((1, s, LANES), lambda b, h: (b, 0, h)),
                  pl.BlockSpec(lam.shape, lambda b, h: (0, 0)),
                  pl.BlockSpec((1, LANES), lambda b, h: (0, 0))],
        out_specs=pl.BlockSpec((1, s, LANES), lambda b, h: (b, 0, h)),
        compiler_params=_cparams(2),
        name="diff_attention",
    )(q, k, v, lam.astype(F32), subln_g.reshape(1, -1))


def kernel(x, c, ln_g, ln_b, w_ada, b_ada, w_ffn_in, w_ffn_out, dsa_w_in, dsa_kv_norm, dsa_w_kv_up, dsa_w_out, fox_w_in, fox_f_bias, fox_w_out, swa_w_in, swa_sinks, swa_w_out, diff_w_in, diff_lambda, diff_subln, diff_w_out):
    bsz, s, d = x.shape
    depth = w_ada.shape[0]
    n_mixers = 4
    alpha = (2 * depth) ** 0.25
    tm = min(512, s)
    tabs = _rope_lane_tables(s, ROT_DIM, HEAD_DIM)
    tabs_i = _rope_lane_tables(s, A_IDX_DIM // 4, A_IDX_DIM)
    mods = _ada_call(c, w_ada, b_ada).reshape(depth, bsz, N_ADA, d)
    w_in_bf, w_out_bf = w_ffn_in.astype(BF16), w_ffn_out.astype(BF16)
    for i in range(depth):
        m, r = i % n_mixers, i // n_mixers
        mod = mods[i]
        x = _ffn_call(x, mod, w_in_bf, w_out_bf, (i, 0), ln_g[i, 0], ln_b[i, 0], j=0, alpha=alpha)
        if m == 0:
            a = _mixer_dsa(x, mod, dsa_w_in[r], dsa_kv_norm[r], dsa_w_kv_up[r], tabs, tabs_i, tm=tm)
            w_out = dsa_w_out[r]
        elif m == 1:
            a = _mixer_fox(x, mod, fox_w_in[r], fox_f_bias[r], tm=tm)
            w_out = fox_w_out[r]
        elif m == 2:
            a = _mixer_swa(x, mod, swa_w_in[r], swa_sinks[r], tabs, tm=tm)
            w_out = swa_w_out[r]
        else:
            lambda_init = 0.8 - 0.6 * math.exp(-0.3 * i)
            a = _mixer_diff(x, mod, diff_w_in[r], diff_lambda[r], diff_subln[r], tabs, lambda_init, tm=tm)
            w_out = diff_w_out[r]
        x = _mixer_out_ffn_call(a, x, mod, w_out.astype(BF16), ln_g[i, 1], ln_b[i, 1],
                                w_in_bf, w_out_bf, (i, 1), ln_g[i, 2], ln_b[i, 2], alpha=alpha)
    return x
```

```python
import functools
import math

import jax
import jax.numpy as jnp
from jax import lax
from jax.experimental import pallas as pl
from jax.experimental.pallas import tpu as pltpu

F32 = jnp.float32
BF16 = jnp.bfloat16

HEAD_DIM = 64
ROT_DIM = HEAD_DIM // 4
ROPE_THETA = 500000.0
LN_EPS = 1e-5
RMS_EPS = 1e-6
N_ADA = 9
A_KV_RANK = 128
A_IDX_HEADS = 8
A_IDX_DIM = 32
A_TOPK_MAX = 256
C_WINDOW = 128
LANES = 128
NEG = -1e30
INT_MIN = -(2 ** 31)
LOG2E = math.log2(math.e)
Q_SCALE = HEAD_DIM ** -0.5 * LOG2E
VMEM_LIMIT = 56 * 1024 * 1024


def _cparams(n_axes):
    return pltpu.CompilerParams(dimension_semantics=("arbitrary",) * n_axes,
                                vmem_limit_bytes=VMEM_LIMIT)


def _resident(shape):
    nd = len(shape)
    return pl.BlockSpec(shape, lambda *_: (0,) * nd, pipeline_mode=pl.Buffered(1))


def _resident_layer(stacked, lead):
    rows, cols = stacked.shape[-2:]
    return pl.BlockSpec((None,) * len(lead) + (rows, cols), lambda *_: tuple(lead) + (0, 0),
                        pipeline_mode=pl.Buffered(1))


def _dot(a, b):
    return jnp.dot(a, b, preferred_element_type=F32)


def _dot_nt(a, b):
    return lax.dot_general(a, b, (((1,), (1,)), ((), ())), preferred_element_type=F32)


def _sigmoid(x):
    return 1.0 / (1.0 + jnp.exp(-x))


def _layer_norm(z, g, b):
    mu = jnp.mean(z, axis=-1, keepdims=True)
    zc = z - mu
    var = jnp.mean(zc * zc, axis=-1, keepdims=True)
    return zc * lax.rsqrt(var + LN_EPS) * g + b


def _modulate(x, mod_ref, j):
    shift = mod_ref[0, 3 * j:3 * j + 1, :]
    scale = mod_ref[0, 3 * j + 1:3 * j + 2, :]
    return (x * (1.0 + scale) + shift).astype(BF16)


def _rope(xs, c, s_up, s_dn, half):
    return (xs * c + pltpu.roll(xs, LANES - half, 1) * s_up + pltpu.roll(xs, half, 1) * s_dn)


def _split_bf16(x):
    hi = x.astype(BF16)
    return hi, (x - hi.astype(F32)).astype(BF16)


def _ada_kernel(c_ref, w_ref, b_ref, o_ref):
    c = c_ref[...]
    c_hi, c_lo = _split_bf16(c * _sigmoid(c))
    w_hi, w_lo = _split_bf16(w_ref[0])
    o_ref[0] = _dot(c_hi, w_hi) + (_dot(c_hi, w_lo) + _dot(c_lo, w_hi)) + b_ref[0]


def _ada_call(c, w_ada, b_ada):
    depth, d, nd = w_ada.shape
    b = c.shape[0]
    n_blk = nd // d
    return pl.pallas_call(
        _ada_kernel,
        out_shape=jax.ShapeDtypeStruct((depth, b, nd), F32),
        grid=(depth, n_blk),
        in_specs=[pl.BlockSpec((b, d), lambda l, n: (0, 0)),
                  pl.BlockSpec((1, d, d), lambda l, n: (l, 0, n)),
                  pl.BlockSpec((1, 1, d), lambda l, n: (l, 0, n))],
        out_specs=pl.BlockSpec((1, b, d), lambda l, n: (l, 0, n)),
        compiler_params=_cparams(2),
        name="ada_mod",
    )(c, w_ada, b_ada.reshape(depth, 1, nd))


FFN_ROWS = 1024
FFN_SUB = 512
FFN_CHUNK = 256


def _swiglu_rows(h, win_ref, wout_ref, dff):
    y = jnp.zeros((h.shape[0], wout_ref.shape[1]), F32)
    a_prev = None
    for c0 in range(0, dff, FFN_CHUNK):
        g = _dot(h, win_ref[:, c0:c0 + FFN_CHUNK])
        u = _dot(h, win_ref[:, dff + c0:dff + c0 + FFN_CHUNK])
        if a_prev is not None:
            y = y + _dot(a_prev, wout_ref[c0 - FFN_CHUNK:c0, :])
        a_prev = (g * _sigmoid(g) * u).astype(BF16)
    return y + _dot(a_prev, wout_ref[dff - FFN_CHUNK:dff, :])


def _ffn_kernel(x_ref, mod_ref, win_ref, wout_ref, g_ref, b_ref, o_ref, *, j, dff, alpha):
    gate = mod_ref[0, 3 * j + 2:3 * j + 3, :]
    for r0 in range(0, x_ref.shape[1], FFN_SUB):
        x = x_ref[0, r0:r0 + FFN_SUB, :]
        y = _swiglu_rows(_modulate(x, mod_ref, j), win_ref, wout_ref, dff)
        z = alpha * x + 0.5 * (1.0 + gate) * y
        o_ref[0, r0:r0 + FFN_SUB, :] = _layer_norm(z, g_ref[...], b_ref[...])


def _mixer_out_ffn_kernel(a_ref, x_ref, mod_ref, wo_ref, g1_ref, b1_ref, win_ref, wout_ref, g2_ref, b2_ref,
                          o_ref, *, dff, alpha):
    subs = range(0, x_ref.shape[1], FFN_SUB)
    ys = [_dot(a_ref[0, r0:r0 + FFN_SUB, :], wo_ref[...]) for r0 in subs]
    gate1 = mod_ref[0, 5:6, :]
    gate2 = mod_ref[0, 8:9, :]
    for r0, y1 in zip(subs, ys):
        x = _layer_norm(alpha * x_ref[0, r0:r0 + FFN_SUB, :] + (1.0 + gate1) * y1, g1_ref[...], b1_ref[...])
        y = _swiglu_rows(_modulate(x, mod_ref, 2), win_ref, wout_ref, dff)
        z = alpha * x + 0.5 * (1.0 + gate2) * y
        o_ref[0, r0:r0 + FFN_SUB, :] = _layer_norm(z, g2_ref[...], b2_ref[...])


def _ffn_rows(s):
    return min(FFN_ROWS, s)


def _ffn_call(x, mod, w_in, w_out, lead, g, b, *, j, alpha):
    bsz, s, d = x.shape
    dff = w_out.shape[-2]
    tm = _ffn_rows(s)
    return pl.pallas_call(
        functools.partial(_ffn_kernel, j=j, dff=dff, alpha=alpha),
        out_shape=jax.ShapeDtypeStruct(x.shape, F32),
        grid=(bsz, s // tm),
        in_specs=[pl.BlockSpec((1, tm, d), lambda bi, si: (bi, si, 0)),
                  pl.BlockSpec((1, N_ADA, d), lambda bi, si: (bi, 0, 0)),
                  _resident_layer(w_in, lead), _resident_layer(w_out, lead),
                  _resident((1, d)), _resident((1, d))],
        out_specs=pl.BlockSpec((1, tm, d), lambda bi, si: (bi, si, 0)),
        compiler_params=_cparams(2),
        name="ffn_sublayer",
    )(x, mod, w_in, w_out, g.reshape(1, d), b.reshape(1, d))


def _mixer_out_ffn_call(a, x, mod, w_o, g1, b1, w_in, w_out, lead, g2, b2, *, alpha):
    bsz, s, d = x.shape
    dff = w_out.shape[-2]
    tm = _ffn_rows(s)
    row_block = lambda width: pl.BlockSpec((1, tm, width), lambda bi, si: (bi, si, 0))
    return pl.pallas_call(
        functools.partial(_mixer_out_ffn_kernel, dff=dff, alpha=alpha),
        out_shape=jax.ShapeDtypeStruct(x.shape, F32),
        grid=(bsz, s // tm),
        in_specs=[row_block(a.shape[-1]), row_block(d),
                  pl.BlockSpec((1, N_ADA, d), lambda bi, si: (bi, 0, 0)),
                  _resident(w_o.shape), _resident((1, d)), _resident((1, d)),
                  _resident_layer(w_in, lead), _resident_layer(w_out, lead),
                  _resident((1, d)), _resident((1, d))],
        out_specs=row_block(d),
        compiler_params=_cparams(2),
        name="mixer_out_ffn",
    )(a, x, mod, w_o, g1.reshape(1, d), b1.reshape(1, d), w_in, w_out, g2.reshape(1, d), b2.reshape(1, d))


def _rope_store(dst_ref, p, col0, n_blk, tabs, half, scale=None):
    c, s_up, s_dn = tabs
    for i in range(n_blk):
        xs = _rope(p[:, col0 + i * LANES:col0 + (i + 1) * LANES], c, s_up, s_dn, half)
        if scale is not None:
            xs = xs * scale
        dst_ref[0, :, i * LANES:(i + 1) * LANES] = xs.astype(dst_ref.dtype)


def _plain_store(dst_ref, p, col0, n_blk):
    dst_ref[0] = p[:, col0:col0 + n_blk * LANES].astype(dst_ref.dtype)


def _dsa_proj_kernel(x_ref, mod_ref, w_ref, kvn_ref, wkv_ref, c_ref, su_ref, sd_ref,
                     ci_ref, sui_ref, sdi_ref,
                     q_ref, k_ref, v_ref, qi_ref, ki_ref, wi_ref, *, d):
    h = _modulate(x_ref[0], mod_ref, 1)
    p = _dot(h, w_ref[...])
    tabs = (c_ref[...], su_ref[...], sd_ref[...])
    tabs_i = (ci_ref[...], sui_ref[...], sdi_ref[...])
    nq = d // LANES
    _rope_store(q_ref, p, 0, nq, tabs, ROT_DIM // 2, scale=Q_SCALE)
    ckv = p[:, d:d + A_KV_RANK]
    ckv = ckv * lax.rsqrt(jnp.mean(ckv * ckv, axis=-1, keepdims=True) + RMS_EPS) * kvn_ref[...]
    kv = _dot(ckv.astype(BF16), wkv_ref[...])
    _rope_store(k_ref, kv, 0, 1, tabs, ROT_DIM // 2)
    _plain_store(v_ref, kv, LANES, 1)
    c0 = d + A_KV_RANK
    _rope_store(qi_ref, p, c0, 2, tabs_i, A_IDX_DIM // 8)
    _rope_store(ki_ref, p, c0 + 2 * LANES, 2, tabs_i, A_IDX_DIM // 8)
    wi_ref[0] = p[:, c0 + 4 * LANES:c0 + 5 * LANES] * (A_IDX_HEADS ** -0.5 * A_IDX_DIM ** -0.5)


def _fox_proj_kernel(x_ref, mod_ref, w_ref, fb_ref, q_ref, k_ref, v_ref, cum_ref, carry_ref, *, d, tm):
    h = _modulate(x_ref[0], mod_ref, 1)
    p = _dot(h, w_ref[...])
    q_ref[0] = (p[:, 0:d] * Q_SCALE).astype(BF16)
    k_ref[0] = p[:, d:2 * d].astype(BF16)
    v_ref[0] = p[:, 2 * d:3 * d].astype(BF16)
    f = p[:, 3 * d:3 * d + LANES] + fb_ref[...]
    logf = jnp.minimum(f, 0.0) - jnp.log1p(jnp.exp(-jnp.abs(f)))

    @pl.when(pl.program_id(1) == 0)
    def _():
        carry_ref[...] = jnp.zeros_like(carry_ref)

    hi = logf.astype(BF16)
    r1 = logf - hi.astype(F32)
    mid = r1.astype(BF16)
    lo = (r1 - mid.astype(F32)).astype(BF16)
    row = lax.broadcasted_iota(jnp.int32, (tm, tm), 0)
    col = lax.broadcasted_iota(jnp.int32, (tm, tm), 1)
    tri = (row >= col).astype(BF16)
    cum = _dot(tri, hi) + _dot(tri, mid) + _dot(tri, lo) + carry_ref[...]
    cum_ref[0] = cum * LOG2E
    carry_ref[...] = cum[tm - 1:tm, :]


def _swa_proj_kernel(x_ref, mod_ref, w_ref, c_ref, su_ref, sd_ref, q_ref, k_ref, v_ref, *, d):
    h = _modulate(x_ref[0], mod_ref, 1)
    p = _dot(h, w_ref[...])
    tabs = (c_ref[...], su_ref[...], sd_ref[...])
    nq = d // LANES
    _rope_store(q_ref, p, 0, nq, tabs, ROT_DIM // 2, scale=Q_SCALE)
    _rope_store(k_ref, p, d, 2, tabs, ROT_DIM // 2)
    _plain_store(v_ref, p, d + 2 * LANES, 2)


def _diff_proj_kernel(x_ref, mod_ref, w_ref, c_ref, su_ref, sd_ref, q_ref, k_ref, v_ref, *, d):
    h = _modulate(x_ref[0], mod_ref, 1)
    p = _dot(h, w_ref[...])
    tabs = (c_ref[...], su_ref[...], sd_ref[...])
    nq = d // LANES
    _rope_store(q_ref, p, 0, nq, tabs, ROT_DIM // 2, scale=Q_SCALE)
    _rope_store(k_ref, p, d, nq, tabs, ROT_DIM // 2)
    _plain_store(v_ref, p, 2 * d, nq)


def _proj_call(kern, x, mod, consts, tabs, outs, *, tm, scratch=(), name):
    bsz, s, d = x.shape
    in_specs = [pl.BlockSpec((1, tm, d), lambda bi, si: (bi, si, 0)),
                pl.BlockSpec((1, N_ADA, d), lambda bi, si: (bi, 0, 0))]
    in_specs += [_resident(a.shape) for a in consts]
    in_specs += [pl.BlockSpec((tm, LANES), lambda bi, si: (si, 0)) for _ in tabs]
    return pl.pallas_call(
        kern,
        out_shape=[jax.ShapeDtypeStruct((bsz, s, w), dt) for w, dt in outs],
        grid=(bsz, s // tm),
        in_specs=in_specs,
        out_specs=[pl.BlockSpec((1, tm, w), lambda bi, si: (bi, si, 0)) for w, _ in outs],
        scratch_shapes=list(scratch),
        compiler_params=_cparams(2),
        name=name,
    )(x, mod, *consts, *tabs)


def _lane_is_low(shape):
    return lax.broadcasted_iota(jnp.int32, shape, 1) < HEAD_DIM


def _split_pair(q):
    low = _lane_is_low(q.shape)
    zero = jnp.zeros_like(q)
    return jnp.where(low, q, zero), jnp.where(low, zero, q)


def _lane_tile(x, width):
    return jnp.concatenate([x] * (width // LANES), axis=1)


def _softmax_tile(s, m, row_term=None):
    tile_max = jnp.broadcast_to(jnp.max(s, axis=-1, keepdims=True), m.shape)
    if row_term is not None:
        tile_max = tile_max + row_term
    m_new = jnp.maximum(m, tile_max)
    shift = m_new if row_term is None else m_new - row_term
    p = jnp.exp2(s - _lane_tile(shift, s.shape[1])).astype(BF16)
    return p, jnp.exp2(m - m_new), m_new


def _causal_mask(n):
    row = lax.broadcasted_iota(jnp.int32, (n, n), 0)
    col = lax.broadcasted_iota(jnp.int32, (n, n), 1)
    return row >= col


def _flash_block(first, count, mask_last, logits, consume, st):
    s_prev = logits(first, mask_last and count == 1)
    for k in range(1, count):
        s_next = logits(first + k, mask_last and k == count - 1)
        st = consume(first + k - 1, s_prev, st)
        s_prev = s_next
    return consume(first + count - 1, s_prev, st)


def _flash_tiles(n_tiles, max_tiles, logits, consume, state):
    branches = [functools.partial(_flash_block, 0, c, False, logits, consume) for c in range(1, max_tiles + 1)]
    return lax.switch(n_tiles - 1, branches, state)


def _pair_values(vj):
    low = _lane_is_low(vj.shape)
    one = jnp.ones_like(vj)
    return jnp.where(low, vj, one), jnp.where(low, one, vj)


def _pair_init(t):
    m = jnp.full((t, LANES), NEG, F32)
    zacc = jnp.zeros((t, LANES), F32)
    return m, m, zacc, zacc


def _pair_consume(s, state, v0, v1, row_terms=(None, None)):
    m0, m1, acc0, acc1 = state
    p0, a0, m0 = _softmax_tile(s[0], m0, row_terms[0])
    p1, a1, m1 = _softmax_tile(s[1], m1, row_terms[1])
    return m0, m1, a0 * acc0 + _dot(p0, v0), a1 * acc1 + _dot(p1, v1)


def _pair_finish(acc0, acc1):
    low = _lane_is_low(acc0.shape)
    return jnp.where(low, acc0 / pltpu.roll(acc0, HEAD_DIM, 1), acc1 / pltpu.roll(acc1, HEAD_DIM, 1))


def _key_tile(ref, j, t):
    if isinstance(j, int):
        return ref[0, j * t:(j + 1) * t, :]
    return ref[0, pl.ds(pl.multiple_of(j * t, t), t), :]


def _fox_attn_kernel(q_ref, k_ref, v_ref, ck_ref, o_ref, *, t):
    for i in range(k_ref.shape[1] // t):
        q0, q1 = _split_pair(_key_tile(q_ref, i, t))
        cq = ck_ref[0, 0, i]
        cq0 = jnp.transpose(jnp.broadcast_to(cq[0:1, :], (LANES, t)))
        cq1 = jnp.transpose(jnp.broadcast_to(cq[1:2, :], (LANES, t)))

        def logits(j, masked):
            kj = _key_tile(k_ref, j, t)
            ck = ck_ref[0, 0, j]
            s0 = _dot_nt(q0, kj) - ck[0:1, :]
            s1 = _dot_nt(q1, kj) - ck[1:2, :]
            if masked:
                ok = _causal_mask(t)
                s0 = jnp.where(ok, s0, NEG)
                s1 = jnp.where(ok, s1, NEG)
            return s0, s1

        def consume(j, s, state):
            v0, v1 = _pair_values(_key_tile(v_ref, j, t))
            return _pair_consume(s, state, v0, v1, (cq0, cq1))

        _, _, acc0, acc1 = _flash_block(0, i + 1, True, logits, consume, _pair_init(t))
        o_ref[0, i * t:(i + 1) * t, :] = _pair_finish(acc0, acc1).astype(o_ref.dtype)


def _diff_attn_kernel(q_ref, k_ref, v_ref, lam_ref, g_ref, o_ref, *, t, lambda_init):
    lam = lam_ref[...]
    lam_val = (jnp.exp(jnp.sum(lam[0:1] * lam[1:2], axis=-1, keepdims=True))
               - jnp.exp(jnp.sum(lam[2:3] * lam[3:4], axis=-1, keepdims=True)) + lambda_init)
    m_init = jnp.full((t, LANES), NEG, F32)
    zacc = jnp.zeros((t, 2 * LANES), F32)
    for i in range(k_ref.shape[1] // t):
        q0, q1 = _split_pair(_key_tile(q_ref, i, t))

        def logits(j, masked):
            kj = _key_tile(k_ref, j, t)
            s0 = _dot_nt(q0, kj)
            s1 = _dot_nt(q1, kj)
            if masked:
                ok = _causal_mask(t)
                s0 = jnp.where(ok, s0, NEG)
                s1 = jnp.where(ok, s1, NEG)
            return s0, s1

        def consume(j, s, state):
            m0, m1, acc0, acc1 = state
            vj = _key_tile(v_ref, j, t)
            v_ext = jnp.concatenate([vj, jnp.ones_like(vj)], axis=1)
            p0, a0, m0 = _softmax_tile(s[0], m0)
            p1, a1, m1 = _softmax_tile(s[1], m1)
            acc0 = _lane_tile(a0, 2 * LANES) * acc0 + _dot(p0, v_ext)
            acc1 = _lane_tile(a1, 2 * LANES) * acc1 + _dot(p1, v_ext)
            return m0, m1, acc0, acc1

        _, _, acc0, acc1 = _flash_block(0, i + 1, True, logits, consume, (m_init, m_init, zacc, zacc))
        out = acc0[:, :LANES] / acc0[:, LANES:] - lam_val * (acc1[:, :LANES] / acc1[:, LANES:])
        out = out * lax.rsqrt(jnp.mean(out * out, axis=-1, keepdims=True) + RMS_EPS) * g_ref[...]
        o_ref[0, i * t:(i + 1) * t, :] = (out * (1.0 - lambda_init)).astype(o_ref.dtype)


def _swa_attn_kernel(sink_ref, q_ref, k_ref, v_ref, o_ref, *, t, blk):
    hp = pl.program_id(1)
    low = _lane_is_low((blk, LANES))
    sinks = (sink_ref[2 * hp] * LOG2E, sink_ref[2 * hp + 1] * LOG2E)
    row = lax.broadcasted_iota(jnp.int32, (blk, 2 * blk), 0)
    col = lax.broadcasted_iota(jnp.int32, (blk, 2 * blk), 1)

    def window(offset):
        dist = row + offset - col
        return (dist >= 0) & (dist < C_WINDOW)

    ok_first, ok_rest = window(0), window(blk)
    for g0 in range(0, q_ref.shape[1], t):
        pending = []
        for r0 in range(g0, g0 + t, blk):
            k0 = max(r0 - blk, 0)
            kj = k_ref[0, k0:k0 + 2 * blk, :]
            ok = ok_first if r0 == 0 else ok_rest
            logits = [jnp.where(ok, _dot_nt(qe, kj), NEG) for qe in _split_pair(q_ref[0, r0:r0 + blk, :])]
            pending.append((r0, k0, logits))
        for r0, k0, logits in pending:
            vj = v_ref[0, k0:k0 + 2 * blk, :]
            outs = []
            for s, sink in zip(logits, sinks):
                m = jnp.maximum(jnp.max(s, axis=-1, keepdims=True), sink)
                p = jnp.exp2(s - m)
                denom = jnp.sum(p, axis=-1, keepdims=True) + jnp.exp2(sink - m)
                outs.append(_dot(p.astype(BF16), vj) / denom)
            o_ref[0, r0:r0 + blk, :] = jnp.where(low, outs[0], outs[1]).astype(o_ref.dtype)


def _sortable(score):
    bits = lax.bitcast_convert_type(score, jnp.int32)
    return bits ^ (lax.shift_right_arithmetic(bits, 31) & jnp.int32(0x7FFFFFFF))


def _sublane_fold(x):
    return jnp.sum(x.reshape(x.shape[0] // 8, 8, x.shape[1]), axis=0)


def _dsa_attn_kernel(q_ref, k_ref, v_ref, qi_ref, ki_ref, wi_ref, o_ref, keys_ref, selb_ref, *, t, k_sel):
    i = pl.program_id(1)
    hp = pl.program_id(2)

    def count_keys(pred_fn):
        def body(j, acc):
            return acc + _sublane_fold(jnp.where(pred_fn(keys_ref[j]), 1.0, 0.0))
        acc = lax.fori_loop(0, i + 1, body, jnp.zeros((8, t), F32))
        return jnp.sum(acc, axis=0, keepdims=True)

    @pl.when(hp == 0)
    def _select():
        qi = qi_ref[0]
        lane = lax.broadcasted_iota(jnp.int32, qi.shape, 1)
        qis = [jnp.where((lane >= h * A_IDX_DIM) & (lane < (h + 1) * A_IDX_DIM), qi, jnp.zeros_like(qi))
               for h in range(A_IDX_HEADS)]
        w_t = jnp.transpose(wi_ref[0])
        key_pos = lax.broadcasted_iota(jnp.int32, (t, t), 0)
        qry_pos = lax.broadcasted_iota(jnp.int32, (t, t), 1)

        def score_tile(j, masked):
            kij = ki_ref[0, pl.ds(pl.multiple_of(j * t, t), t), :]
            score = jnp.zeros((t, t), F32)
            for h in range(A_IDX_HEADS):
                score = score + w_t[h:h + 1, :] * jnp.maximum(_dot_nt(kij, qis[h]), 0.0)
            key = _sortable(score)
            if masked:
                key = jnp.where(key_pos <= qry_pos, key, INT_MIN)
            keys_ref[j] = key

        def score_body(j, c):
            score_tile(j, False)
            return c
        lax.fori_loop(0, i, score_body, 0)
        score_tile(i, True)

        def bit_body(b, thr_u):
            cand = thr_u | lax.shift_left(jnp.int32(1), 31 - b)
            cand_s = cand ^ INT_MIN
            cnt = count_keys(lambda key: key >= cand_s)
            return jnp.where(cnt >= k_sel, cand, thr_u)
        thr_u = lax.fori_loop(0, 32, bit_body, jnp.zeros((1, t), jnp.int32))
        thr = thr_u ^ INT_MIN
        need = k_sel - count_keys(lambda key: key > thr)

        earlier = (qry_pos < key_pos).astype(BF16)

        def sel_tile(j, seen, masked):
            key = keys_ref[j]
            eq = key == thr
            eq_f = jnp.where(eq, 1.0, 0.0)
            rank = _dot(earlier, eq_f.astype(BF16)) + seen
            sel = (key > thr) | (eq & (rank < need))
            if masked:
                sel = sel & (key_pos <= qry_pos)
            selb_ref[j] = jnp.transpose(jnp.where(sel, 0.0, NEG))
            return seen + jnp.sum(_sublane_fold(eq_f), axis=0, keepdims=True)

        seen = lax.fori_loop(0, i, lambda j, c: sel_tile(j, c, False), jnp.zeros((1, t), F32))
        sel_tile(i, seen, True)

    q0, q1 = _split_pair(q_ref[0])

    def logits(j, _):
        kj = _key_tile(k_ref, j, t)
        bias = selb_ref[j]
        return _dot_nt(q0, kj) + bias, _dot_nt(q1, kj) + bias

    def consume(j, s, state):
        v0, v1 = _pair_values(_key_tile(v_ref, j, t))
        return _pair_consume(s, state, v0, v1)

    _, _, acc0, acc1 = _flash_tiles(i + 1, k_ref.shape[1] // t, logits, consume, _pair_init(t))
    o_ref[0] = _pair_finish(acc0, acc1).astype(o_ref.dtype)


def _rope_lane_tables(seq, rot_dim, period):
    half = rot_dim // 2
    inv = ROPE_THETA ** (-jnp.arange(0, rot_dim, 2, dtype=F32) / rot_dim)
    ang = jnp.arange(seq, dtype=F32)[:, None] * inv[None, :]
    cos, sin = jnp.cos(ang), jnp.sin(ang)
    lane = jnp.arange(LANES) % period
    idx = jnp.where(lane < half, lane, jnp.clip(lane - half, 0, half - 1))
    first = (lane < half)[None, :]
    second = ((lane >= half) & (lane < 2 * half))[None, :]
    c = jnp.where(first | second, cos[:, idx], 1.0)
    s_up = jnp.where(first, -sin[:, idx], 0.0)
    s_dn = jnp.where(second, sin[:, idx], 0.0)
    return c.astype(F32), s_up.astype(F32), s_dn.astype(F32)


def _pad_cols(w, n):
    return jnp.pad(w, ((0, 0), (0, n - w.shape[1])))


def _attn_tile(s):
    return min(512, s)


def _mixer_dsa(x, mod, w_in, kv_norm, w_kv_up, tabs, tabs_i, *, tm):
    bsz, s, d = x.shape
    t = _attn_tile(s)
    k_sel = min(A_TOPK_MAX, s // 4)
    n_qi = A_IDX_HEADS * A_IDX_DIM
    c0 = d + A_KV_RANK
    w_ki = w_in[:, c0 + n_qi:c0 + n_qi + A_IDX_DIM]
    w_wi = w_in[:, c0 + n_qi + A_IDX_DIM:]
    w_cat = jnp.concatenate([w_in[:, :c0 + n_qi], jnp.tile(w_ki, (1, n_qi // A_IDX_DIM)),
                             _pad_cols(w_wi, LANES)], axis=1).astype(BF16)
    wk, wv = w_kv_up[:, :HEAD_DIM], w_kv_up[:, HEAD_DIM:]
    wkv = jnp.concatenate([wk, wk, wv, wv], axis=1).astype(BF16)
    q, k2, v2, qi, ki, wi = _proj_call(
        functools.partial(_dsa_proj_kernel, d=d), x, mod,
        [w_cat, kv_norm.reshape(1, -1), wkv], list(tabs) + list(tabs_i),
        [(d, BF16), (LANES, BF16), (LANES, BF16), (n_qi, BF16), (n_qi, BF16), (LANES, F32)],
        tm=tm, name="dsa_in_proj")
    nt = s // t
    return pl.pallas_call(
        functools.partial(_dsa_attn_kernel, t=t, k_sel=k_sel),
        out_shape=jax.ShapeDtypeStruct((bsz, s, d), BF16),
        grid=(bsz, nt, d // LANES),
        in_specs=[pl.BlockSpec((1, t, LANES), lambda b, i, h: (b, i, h)),
                  pl.BlockSpec((1, s, LANES), lambda b, i, h: (b, 0, 0)),
                  pl.BlockSpec((1, s, LANES), lambda b, i, h: (b, 0, 0)),
                  pl.BlockSpec((1, t, n_qi), lambda b, i, h: (b, i, 0)),
                  pl.BlockSpec((1, s, n_qi), lambda b, i, h: (b, 0, 0)),
                  pl.BlockSpec((1, t, LANES), lambda b, i, h: (b, i, 0))],
        out_specs=pl.BlockSpec((1, t, LANES), lambda b, i, h: (b, i, h)),
        scratch_shapes=[pltpu.VMEM((nt, t, t), jnp.int32), pltpu.VMEM((nt, t, t), F32)],
        compiler_params=_cparams(3),
        name="dsa_attention",
    )(q, k2, v2, qi, ki, wi)


def _mixer_fox(x, mod, w_in, f_bias, *, tm):
    bsz, s, d = x.shape
    t = _attn_tile(s)
    nh = d // HEAD_DIM
    w_cat = _pad_cols(w_in, 3 * d + LANES).astype(BF16)
    fb = _pad_cols(f_bias.reshape(1, -1), LANES)
    q, k, v, cum = _proj_call(
        functools.partial(_fox_proj_kernel, d=d, tm=tm), x, mod, [w_cat, fb], [],
        [(d, BF16), (d, BF16), (d, BF16), (LANES, F32)],
        tm=tm, scratch=[pltpu.VMEM((1, LANES), F32)], name="fox_in_proj")
    nt = s // t
    ck = jnp.transpose(cum[:, :, :nh].reshape(bsz, nt, t, nh // 2, 2), (0, 3, 1, 4, 2))
    return pl.pallas_call(
        functools.partial(_fox_attn_kernel, t=t),
        out_shape=jax.ShapeDtypeStruct((bsz, s, d), BF16),
        grid=(bsz, d // LANES),
        in_specs=[pl.BlockSpec((1, s, LANES), lambda b, h: (b, 0, h)),
                  pl.BlockSpec((1, s, LANES), lambda b, h: (b, 0, h)),
                  pl.BlockSpec((1, s, LANES), lambda b, h: (b, 0, h)),
                  pl.BlockSpec((1, 1, nt, 2, t), lambda b, h: (b, h, 0, 0, 0))],
        out_specs=pl.BlockSpec((1, s, LANES), lambda b, h: (b, 0, h)),
        compiler_params=_cparams(2),
        name="fox_attention",
    )(q, k, v, ck)


def _mixer_swa(x, mod, w_in, sinks, tabs, *, tm):
    bsz, s, d = x.shape
    t = _attn_tile(s)
    n_kv = (w_in.shape[1] - d) // (2 * HEAD_DIM)
    pairs_per_kv = (d // LANES) // n_kv
    wq, wk, wv = w_in[:, :d], w_in[:, d:d + n_kv * HEAD_DIM], w_in[:, d + n_kv * HEAD_DIM:]

    def dup(w):
        w = w.reshape(w.shape[0], n_kv, 1, HEAD_DIM)
        return jnp.broadcast_to(w, (w.shape[0], n_kv, 2, HEAD_DIM)).reshape(w.shape[0], n_kv * LANES)

    w_cat = jnp.concatenate([wq, dup(wk), dup(wv)], axis=1).astype(BF16)
    q, k2, v2 = _proj_call(
        functools.partial(_swa_proj_kernel, d=d), x, mod, [w_cat], list(tabs),
        [(d, BF16), (n_kv * LANES, BF16), (n_kv * LANES, BF16)], tm=tm, name="swa_in_proj")
    return pl.pallas_call(
        functools.partial(_swa_attn_kernel, t=t, blk=C_WINDOW),
        out_shape=jax.ShapeDtypeStruct((bsz, s, d), BF16),
        grid=(bsz, d // LANES),
        in_specs=[pl.BlockSpec(memory_space=pltpu.SMEM),
                  pl.BlockSpec((1, s, LANES), lambda b, h: (b, 0, h)),
                  pl.BlockSpec((1, s, LANES), lambda b, h: (b, 0, h // pairs_per_kv)),
                  pl.BlockSpec((1, s, LANES), lambda b, h: (b, 0, h // pairs_per_kv))],
        out_specs=pl.BlockSpec((1, s, LANES), lambda b, h: (b, 0, h)),
        compiler_params=_cparams(2),
        name="swa_attention",
    )(sinks.astype(F32), q, k2, v2)


def _mixer_diff(x, mod, w_in, lam, subln_g, tabs, lambda_init, *, tm):
    bsz, s, d = x.shape
    t = _attn_tile(s)
    q, k, v = _proj_call(
        functools.partial(_diff_proj_kernel, d=d), x, mod, [w_in.astype(BF16)], list(tabs),
        [(d, BF16), (d, BF16), (d, BF16)], tm=tm, name="diff_in_proj")
    return pl.pallas_call(
        functools.partial(_diff_attn_kernel, t=t, lambda_init=lambda_init),
        out_shape=jax.ShapeDtypeStruct((bsz, s, d), BF16),
        grid=(bsz, d // LANES),
        in_specs=[pl.BlockSpec((1, s, LANES), lambda b, h: (b, 0, h)),
                  pl.BlockSpec((1, s, LANES), lambda b, h: (b, 0, h)),
                  pl.BlockSpec((1, s, LANES), lambda b, h: (b, 0, h)),
                  pl.BlockSpec(lam.shape, lambda b, h: (0, 0)),
                  pl.BlockSpec((1, LANES), lambda b, h: (0, 0))],
        out_specs=pl.BlockSpec((1, s, LANES), lambda b, h: (b, 0, h)),
        compiler_params=_cparams(2),
        name="diff_attention",
    )(q, k, v, lam.astype(F32), subln_g.reshape(1, -1))


def kernel(x, c, ln_g, ln_b, w_ada, b_ada, w_ffn_in, w_ffn_out, dsa_w_in, dsa_kv_norm, dsa_w_kv_up, dsa_w_out, fox_w_in, fox_f_bias, fox_w_out, swa_w_in, swa_sinks, swa_w_out, diff_w_in, diff_lambda, diff_subln, diff_w_out):
    bsz, s, d = x.shape
    depth = w_ada.shape[0]
    n_mixers = 4
    alpha = (2 * depth) ** 0.25
    tm = min(512, s)
    tabs = _rope_lane_tables(s, ROT_DIM, HEAD_DIM)
    tabs_i = _rope_lane_tables(s, A_IDX_DIM // 4, A_IDX_DIM)
    mods = _ada_call(c, w_ada, b_ada).reshape(depth, bsz, N_ADA, d)
    w_in_bf, w_out_bf = w_ffn_in.astype(BF16), w_ffn_out.astype(BF16)
    for i in range(depth):
        m, r = i % n_mixers, i // n_mixers
        mod = mods[i]
        x = _ffn_call(x, mod, w_in_bf, w_out_bf, (i, 0), ln_g[i, 0], ln_b[i, 0], j=0, alpha=alpha)
        if m == 0:
            a = _mixer_dsa(x, mod, dsa_w_in[r], dsa_kv_norm[r], dsa_w_kv_up[r], tabs, tabs_i, tm=tm)
            w_out = dsa_w_out[r]
        elif m == 1:
            a = _mixer_fox(x, mod, fox_w_in[r], fox_f_bias[r], tm=tm)
            w_out = fox_w_out[r]
        elif m == 2:
            a = _mixer_swa(x, mod, swa_w_in[r], swa_sinks[r], tabs, tm=tm)
            w_out = swa_w_out[r]
        else:
            lambda_init = 0.8 - 0.6 * math.exp(-0.3 * i)
            a = _mixer_diff(x, mod, diff_w_in[r], diff_lambda[r], diff_subln[r], tabs, lambda_init, tm=tm)
            w_out = diff_w_out[r]
        x = _mixer_out_ffn_call(a, x, mod, w_out.astype(BF16), ln_g[i, 1], ln_b[i, 1],
                                w_in_bf, w_out_bf, (i, 1), ln_g[i, 2], ln_b[i, 2], alpha=alpha)
    return x
```
